```python
import math
import jax, jax.numpy as jnp
from jax import lax
import numpy as np

D_MODEL = 2048
BATCH = 4
SEQ = 4096
DEPTH = 2

POOL_WINDOWS = (2, 4, 8, 16)
N_POOL_GROUPS = len(POOL_WINDOWS)
POOL_GROUP_DIM = D_MODEL // N_POOL_GROUPS
HEAD_DIM = 128
N_HEADS = D_MODEL // HEAD_DIM
DILATED_BRANCHES = ((128, 1), (512, 4), (2048, 16))
ATTN_BLOCK = 128
D_FF = 128 * ((8 * D_MODEL // 3 + 127) // 128)
CONV_WIDTH = 3
N_A_LAYERS = DEPTH // 2
N_B_LAYERS = DEPTH - N_A_LAYERS
DEEPNORM_ALPHA = (2.0 * DEPTH) ** 0.25
DEEPNORM_BETA = (8.0 * DEPTH) ** -0.25
LN_EPS = 1e-5
NEG_INF = -1e30

kernel_name = "yoco_pool_dilated_attn_convffn_deepnorm"


def layer_norm(x, g, b):
    xf = x.astype(jnp.float32)
    mu = jnp.mean(xf, axis=-1, keepdims=True)
    var = jnp.mean(jnp.square(xf - mu), axis=-1, keepdims=True)
    y = (xf - mu) * lax.rsqrt(var + LN_EPS) * g.astype(jnp.float32) + b.astype(jnp.float32)
    return y.astype(x.dtype)


def pool_mixer(h, w_in, w_grp, scale, w_out):
    B_, S, D = h.shape
    p = (h @ w_in).reshape(B_, S, N_POOL_GROUPS, POOL_GROUP_DIM).astype(jnp.float32)
    cs = jnp.cumsum(p, axis=1)
    t = jnp.arange(S)
    outs = []
    for g, w in enumerate(POOL_WINDOWS):
        c = cs[:, :, g]
        lag = jnp.pad(c[:, :S - w], ((0, 0), (w, 0), (0, 0)))
        cnt = jnp.minimum(t + 1, w).astype(jnp.float32)[None, :, None]
        outs.append((c - lag) / cnt - p[:, :, g])
    pooled = jnp.stack(outs, axis=2).astype(h.dtype)
    mixed = jnp.einsum('bsgc,gcd->bsgd', pooled, w_grp).reshape(B_, S, D) * scale
    return mixed @ w_out


def _dilated_branch(q, k, v, window, dilation):
    B_, S, H, Dh = q.shape
    L = S // dilation
    wr = window // dilation
    assert wr <= ATTN_BLOCK
    N = B_ * dilation

    def to_res(a):
        return a.reshape(B_, L, dilation, H, Dh).transpose(0, 2, 3, 1, 4).reshape(N, H, L, Dh)

    nb = -(-L // ATTN_BLOCK)
    Lp = nb * ATTN_BLOCK
    qr = jnp.pad(to_res(q), ((0, 0), (0, 0), (0, Lp - L), (0, 0)))
    kr = jnp.pad(to_res(k), ((0, 0), (0, 0), (ATTN_BLOCK, Lp - L), (0, 0)))
    vr = jnp.pad(to_res(v), ((0, 0), (0, 0), (ATTN_BLOCK, Lp - L), (0, 0)))
    qb = qr.reshape(N, H, nb, ATTN_BLOCK, Dh)
    kb = kr.reshape(N, H, nb + 1, ATTN_BLOCK, Dh)
    vb = vr.reshape(N, H, nb + 1, ATTN_BLOCK, Dh)
    kw = jnp.concatenate([kb[:, :, :-1], kb[:, :, 1:]], axis=3)
    vw = jnp.concatenate([vb[:, :, :-1], vb[:, :, 1:]], axis=3)

    s = jnp.einsum('nhbqd,nhbkd->nhbqk', qb, kw).astype(jnp.float32) * (1.0 / math.sqrt(Dh))
    qi = jnp.arange(ATTN_BLOCK)[:, None]
    kj = jnp.arange(2 * ATTN_BLOCK)[None, :]
    dist = ATTN_BLOCK + qi - kj
    band = (dist >= 0) & (dist <= wr)
    key_pos = jnp.arange(nb)[:, None, None] * ATTN_BLOCK - ATTN_BLOCK + kj[None]
    mask = band[None] & (key_pos >= 0)
    s = jnp.where(mask, s, NEG_INF)
    m = jnp.max(s, axis=-1, keepdims=True)
    pr = jnp.exp(s - m)
    den = jnp.sum(pr, axis=-1)
    o = jnp.einsum('nhbqk,nhbkd->nhbqd', pr, vw.astype(jnp.float32)) / den[..., None]
    lse = m[..., 0] + jnp.log(den)

    o = o.reshape(N, H, Lp, Dh)[:, :, :L].reshape(B_, dilation, H, L, Dh)
    o = o.transpose(0, 3, 1, 2, 4).reshape(B_, S, H, Dh)
    lse = lse.reshape(N, H, Lp)[:, :, :L].reshape(B_, dilation, H, L)
    lse = lse.transpose(0, 3, 1, 2).reshape(B_, S, H)
    return o, lse


def dilated_attention(h, k, v, w_q, w_o):
    B_, S, _ = h.shape
    q = (h @ w_q).reshape(B_, S, N_HEADS, HEAD_DIM)
    outs, lses = [], []
    for window, dil in DILATED_BRANCHES:
        o, lse = _dilated_branch(q, k, v, window, dil)
        outs.append(o)
        lses.append(lse)
    wts = jax.nn.softmax(jnp.stack(lses, axis=0), axis=0)
    o = jnp.sum(wts[..., None] * jnp.stack(outs, axis=0), axis=0)
    return o.reshape(B_, S, D_MODEL).astype(h.dtype) @ w_o


def conv_ffn(h, w_up, conv_w, conv_b, w_down):
    S = h.shape[1]
    u = h @ w_up
    up = jnp.pad(u, ((0, 0), (CONV_WIDTH - 1, 0), (0, 0)))
    c = conv_b + up[:, 0:S] * conv_w[0]
    for j in range(1, CONV_WIDTH):
        c = c + up[:, j:j + S] * conv_w[j]
    gate, val = jnp.split(c, 2, axis=-1)
    return (jax.nn.silu(gate) * val) @ w_down


def setup_inputs(seed: int = 0) -> dict:
    key = jax.random.key(seed)
    ks = jax.random.split(key, 20)
    f32 = jnp.float32
    D, G, C, F = D_MODEL, N_POOL_GROUPS, POOL_GROUP_DIM, D_FF

    def nrm(k, shape, scale):
        return jax.random.normal(k, shape, f32) * scale

    return {
        "x": nrm(ks[0], (BATCH, SEQ, D), 1.0),
        "pool_w_in": nrm(ks[1], (N_A_LAYERS, D, D), D ** -0.5),
        "pool_w_grp": nrm(ks[2], (N_A_LAYERS, G, C, C), C ** -0.5),
        "pool_scale": 1.0 + nrm(ks[3], (N_A_LAYERS, D), 0.1),
        "pool_w_out": nrm(ks[4], (N_A_LAYERS, D, D), D ** -0.5 * DEEPNORM_BETA),
        "attn_w_q": nrm(ks[5], (N_B_LAYERS, D, D), D ** -0.5),
        "attn_w_o": nrm(ks[6], (N_B_LAYERS, D, D), D ** -0.5 * DEEPNORM_BETA),
        "shared_w_k": nrm(ks[7], (D, D), D ** -0.5),
        "shared_w_v": nrm(ks[8], (D, D), D ** -0.5 * DEEPNORM_BETA),
        "ffn_w_up": nrm(ks[9], (DEPTH, D, 2 * F), D ** -0.5 * DEEPNORM_BETA),
        "ffn_conv_w": nrm(ks[10], (DEPTH, CONV_WIDTH, 2 * F), CONV_WIDTH ** -0.5),
        "ffn_conv_b": nrm(ks[11], (DEPTH, 2 * F), 0.02),
        "ffn_w_down": nrm(ks[12], (DEPTH, F, D), F ** -0.5 * DEEPNORM_BETA),
        "ln1_g": 1.0 + nrm(ks[13], (DEPTH, D), 0.02),
        "ln1_b": nrm(ks[14], (DEPTH, D), 0.02),
        "ln2_g": 1.0 + nrm(ks[15], (DEPTH, D), 0.02),
        "ln2_b": nrm(ks[16], (DEPTH, D), 0.02),
    }


def reference(x, pool_w_in, pool_w_grp, pool_scale, pool_w_out, attn_w_q, attn_w_o,
              shared_w_k, shared_w_v, ffn_w_up, ffn_conv_w, ffn_conv_b, ffn_w_down,
              ln1_g, ln1_b, ln2_g, ln2_b):
    B_, S, _ = x.shape
    h = x
    k_shared = None
    v_shared = None
    for i in range(DEPTH):
        if i < N_A_LAYERS:
            mix = pool_mixer(h, pool_w_in[i], pool_w_grp[i], pool_scale[i], pool_w_out[i])
        else:
            if i == N_A_LAYERS:
                k_shared = (h @ shared_w_k).reshape(B_, S, N_HEADS, HEAD_DIM)
                v_shared = (h @ shared_w_v).reshape(B_, S, N_HEADS, HEAD_DIM)
            j = i - N_A_LAYERS
            mix = dilated_attention(h, k_shared, v_shared, attn_w_q[j], attn_w_o[j])
        h = layer_norm(DEEPNORM_ALPHA * h + mix, ln1_g[i], ln1_b[i])
        ff = conv_ffn(h, ffn_w_up[i], ffn_conv_w[i], ffn_conv_b[i], ffn_w_down[i])
        h = layer_norm(DEEPNORM_ALPHA * h + ff, ln2_g[i], ln2_b[i])
    return h
```

```python
import functools
import math

import jax
import jax.numpy as jnp
from jax import lax
from jax.experimental import pallas as pl
from jax.experimental.pallas import tpu as pltpu

POOL_WINDOWS = (2, 4, 8, 16)
HEAD_DIM = 128
DILATED_BRANCHES = ((128, 1), (512, 4), (2048, 16))
ATTN_BLOCK = 128
CONV_WIDTH = 3
DEPTH = 2
DEEPNORM_ALPHA = (2.0 * DEPTH) ** 0.25
LN_EPS = 1e-5
NEG_INF = -1e30

LANES = 128
SUBLANES = 8
MXU_DIM = 256
VMEM_LIMIT_BYTES = 56 * 1024 * 1024

MAX_DILATION = max(d for _, d in DILATED_BRANCHES)
POOL_HALO = max(POOL_WINDOWS)

F32 = jnp.float32
BF16 = jnp.bfloat16


def _layer_norm(y, g, b):
    mu = jnp.mean(y, axis=-1, keepdims=True)
    d = y - mu
    var = jnp.mean(d * d, axis=-1, keepdims=True)
    return d * lax.rsqrt(var + LN_EPS) * g + b


def _const_spec(shape):
    nd = len(shape)
    return pl.BlockSpec(shape, lambda *_: (0,) * nd, pipeline_mode=pl.Buffered(1))


def _pool_kernel(x_ref, win_ref, wgrp_ref, scale_ref, wout_ref, g_ref, b_ref, o_ref,
                 pbuf, mbuf, *, tm, blocks_per_seq):
    i = pl.program_id(0)
    blk = i % blocks_per_seq
    d_model = x_ref.shape[1]
    gdim = d_model // len(POOL_WINDOWS)

    @pl.when(blk == 0)
    def _():
        pbuf[0:POOL_HALO, :] = jnp.zeros((POOL_HALO, d_model), F32)

    @pl.when(blk != 0)
    def _():
        pbuf[0:POOL_HALO, :] = pbuf[tm:tm + POOL_HALO, :]

    x = x_ref[...]
    p = jnp.dot(x.astype(BF16), win_ref[...], preferred_element_type=F32)
    pbuf[POOL_HALO:POOL_HALO + tm, :] = p

    pos = blk * tm + lax.broadcasted_iota(jnp.int32, (tm, 1), 0)
    for g, w in enumerate(POOL_WINDOWS):
        cols = slice(g * gdim, (g + 1) * gdim)
        s = pbuf[:, cols]
        shift = 1
        while shift < w:
            s = s + pltpu.roll(s, shift, axis=0)
            shift *= 2
        cnt = jnp.minimum(pos + 1, w).astype(F32)
        pooled = s[POOL_HALO:, :] / cnt - p[:, cols]
        mixed = jnp.dot(pooled.astype(BF16), wgrp_ref[g], preferred_element_type=F32)
        mbuf[:, cols] = (mixed * scale_ref[:, cols]).astype(BF16)

    mix = jnp.dot(mbuf[...], wout_ref[...], preferred_element_type=F32)
    o_ref[...] = _layer_norm(DEEPNORM_ALPHA * x + mix, g_ref[...], b_ref[...])


def _pool_layer(x2d, w_in, w_grp, scale, w_out, ln_g, ln_b, *, seq, tm=256):
    t, d = x2d.shape
    ng, gdim, _ = w_grp.shape
    kern = functools.partial(_pool_kernel, tm=tm, blocks_per_seq=seq // tm)
    return pl.pallas_call(
        kern,
        grid=(t // tm,),
        in_specs=[
            pl.BlockSpec((tm, d), lambda i: (i, 0)),
            _const_spec((d, d)),
            _const_spec((ng, gdim, gdim)),
            _const_spec((1, d)),
            _const_spec((d, d)),
            _const_spec((1, d)),
            _const_spec((1, d)),
        ],
        out_specs=pl.BlockSpec((tm, d), lambda i: (i, 0)),
        out_shape=jax.ShapeDtypeStruct((t, d), F32),
        scratch_shapes=[
            pltpu.VMEM((POOL_HALO + tm, d), F32),
            pltpu.VMEM((tm, d), BF16),
        ],
        compiler_params=pltpu.CompilerParams(
            dimension_semantics=("arbitrary",), vmem_limit_bytes=VMEM_LIMIT_BYTES),
        name="pool_layer",
    )(x2d, w_in, w_grp, scale, w_out, ln_g, ln_b)


def _ffn_kernel(h_ref, wg_ref, wv_ref, cg_ref, cv_ref, wd_ref, g_ref, b_ref, o_ref,
                hb, acc, ug_buf, uv_buf, carry_g, carry_v, *, tm, blocks_per_seq):
    i = pl.program_id(0)
    f = pl.program_id(1)
    nf = pl.num_programs(1)
    seq_start = (i % blocks_per_seq) == 0
    tf = wg_ref.shape[1]

    @pl.when(f == 0)
    def _():
        hb[...] = h_ref[...].astype(BF16)

    ug = jnp.dot(hb[...], wg_ref[...], preferred_element_type=F32)
    uv = jnp.dot(hb[...], wv_ref[...], preferred_element_type=F32)

    @pl.when(seq_start)
    def _():
        ug_buf[0:SUBLANES, :] = jnp.zeros((SUBLANES, tf), F32)
        uv_buf[0:SUBLANES, :] = jnp.zeros((SUBLANES, tf), F32)

    @pl.when(jnp.logical_not(seq_start))
    def _():
        ug_buf[0:SUBLANES, :] = carry_g[f]
        uv_buf[0:SUBLANES, :] = carry_v[f]

    ug_buf[SUBLANES:SUBLANES + tm, :] = ug
    uv_buf[SUBLANES:SUBLANES + tm, :] = uv
    carry_g[f] = ug[tm - SUBLANES:tm, :]
    carry_v[f] = uv[tm - SUBLANES:tm, :]

    def conv(u, buf, cp_ref):
        c = cp_ref[CONV_WIDTH:CONV_WIDTH + 1, :] + u * cp_ref[CONV_WIDTH - 1:CONV_WIDTH, :]
        for lag in range(1, CONV_WIDTH):
            tap = CONV_WIDTH - 1 - lag
            c = c + buf[SUBLANES - lag:SUBLANES - lag + tm, :] * cp_ref[tap:tap + 1, :]
        return c

    gate = conv(ug, ug_buf, cg_ref)
    val = conv(uv, uv_buf, cv_ref)
    act = (gate / (1.0 + jnp.exp(-gate)) * val).astype(BF16)
    part = jnp.dot(act, wd_ref[...], preferred_element_type=F32)

    @pl.when(f == 0)
    def _():
        acc[...] = part

    @pl.when(f != 0)
    def _():
        acc[...] += part

    @pl.when(f == nf - 1)
    def _():
        y = DEEPNORM_ALPHA * h_ref[...] + acc[...]
        o_ref[...] = _layer_norm(y, g_ref[...], b_ref[...])


def _ffn_layer(h2d, w_gate, w_val, cp_gate, cp_val, w_down, ln_g, ln_b, *, seq, tm=512, tf=512):
    t, d = h2d.shape
    fp = w_gate.shape[1]
    nf = fp // tf
    kern = functools.partial(_ffn_kernel, tm=tm, blocks_per_seq=seq // tm)
    return pl.pallas_call(
        kern,
        grid=(t // tm, nf),
        in_specs=[
            pl.BlockSpec((tm, d), lambda i, f: (i, 0)),
            pl.BlockSpec((d, tf), lambda i, f: (0, f)),
            pl.BlockSpec((d, tf), lambda i, f: (0, f)),
            pl.BlockSpec((SUBLANES, tf), lambda i, f: (0, f)),
            pl.BlockSpec((SUBLANES, tf), lambda i, f: (0, f)),
            pl.BlockSpec((tf, d), lambda i, f: (f, 0)),
            _const_spec((1, d)),
            _const_spec((1, d)),
        ],
        out_specs=pl.BlockSpec((tm, d), lambda i, f: (i, 0)),
        out_shape=jax.ShapeDtypeStruct((t, d), F32),
        scratch_shapes=[
            pltpu.VMEM((tm, d), BF16),
            pltpu.VMEM((tm, d), F32),
            pltpu.VMEM((SUBLANES + tm, tf), F32),
            pltpu.VMEM((SUBLANES + tm, tf), F32),
            pltpu.VMEM((nf, SUBLANES, tf), F32),
            pltpu.VMEM((nf, SUBLANES, tf), F32),
        ],
        compiler_params=pltpu.CompilerParams(
            dimension_semantics=("arbitrary", "arbitrary"), vmem_limit_bytes=VMEM_LIMIT_BYTES),
        name="ffn_layer",
    )(h2d, w_gate, w_val, cp_gate, cp_val, w_down, ln_g, ln_b)


def _nat_view(a, width):
    b = a.shape[0]
    return a.reshape(b, -1, MAX_DILATION * width)


def _nat_spec(rows, width, rb):
    return pl.BlockSpec((None, rows, rb * width), lambda b, j: (b, 0, j))


def _d4_view(a, width):
    b = a.shape[0]
    return a.reshape(b, 4, -1, (MAX_DILATION // 4) * width)


def _d4_spec(rows, width, rb):
    per = 4 // rb
    return pl.BlockSpec((None, rb, rows, width), lambda b, j: (b, j % per, 0, j // per))


def _d16_spec(rows, width, rb):
    return pl.BlockSpec((None, rb, rows, width), lambda b, j: (b, j, 0, 0))


def _proj_kernel(x_ref, w_ref, o1_ref, o4_ref, o16_ref, *, rb):
    d = w_ref.shape[0]
    for j in range(rb):
        cols = slice(j * d, (j + 1) * d)
        y = jnp.dot(x_ref[:, cols].astype(BF16), w_ref[...],
                    preferred_element_type=F32).astype(BF16)
        o1_ref[:, cols] = y
        o4_ref[j] = y
        o16_ref[j] = y


def _proj(h3d, w, *, rb=2):
    bsz, seq, d = h3d.shape
    rows = seq // MAX_DILATION
    kern = functools.partial(_proj_kernel, rb=rb)
    o1, o4, o16 = pl.pallas_call(
        kern,
        grid=(bsz, MAX_DILATION // rb),
        in_specs=[_nat_spec(rows, d, rb), pl.BlockSpec((d, d), lambda b, j: (0, 0),
                                                       pipeline_mode=pl.Buffered(1))],
        out_specs=[_nat_spec(rows, d, rb), _d4_spec(rows, d, rb), _d16_spec(rows, d, rb)],
        out_shape=[
            jax.ShapeDtypeStruct((bsz, rows, MAX_DILATION * d), BF16),
            jax.ShapeDtypeStruct((bsz, 4, rows, (MAX_DILATION // 4) * d), BF16),
            jax.ShapeDtypeStruct((bsz, MAX_DILATION, rows, d), BF16),
        ],
        compiler_params=pltpu.CompilerParams(
            dimension_semantics=("arbitrary", "arbitrary"), vmem_limit_bytes=VMEM_LIMIT_BYTES),
        name="proj",
    )(_nat_view(h3d, d), w)
    return (o1.reshape(bsz, 1, seq, d), o4.reshape(bsz, 4, seq // 4, d), o16)


def _attn_kernel(q_ref, kp_ref, kc_ref, vp_ref, vc_ref, o_ref, lse_ref,
                 kbuf, vbuf, bias, *, qb, n_heads):
    i = pl.program_id(2)
    nblk = qb // ATTN_BLOCK
    blk = ATTN_BLOCK

    kbuf[0:blk, :] = kp_ref[...]
    kbuf[blk:blk + qb, :] = kc_ref[...]
    vbuf[0:blk, :] = vp_ref[...]
    vbuf[blk:blk + qb, :] = vc_ref[...]

    row = lax.broadcasted_iota(jnp.int32, (blk, 2 * blk), 0)
    col = lax.broadcasted_iota(jnp.int32, (blk, 2 * blk), 1)
    dist = blk + row - col
    band = (dist >= 0) & (dist <= blk)
    has_prev = jnp.broadcast_to(i > 0, band.shape)
    bias[1] = jnp.where(band, 0.0, NEG_INF)
    bias[0] = jnp.where(band & ((col >= blk) | has_prev), 0.0, NEG_INF)

    scale = 1.0 / math.sqrt(HEAD_DIM)
    lane = lax.broadcasted_iota(jnp.int32, (qb, LANES), 1)

    def head(h, lse_acc):
        c0 = pl.multiple_of(h * HEAD_DIM, HEAD_DIM)
        parts = []
        for j in range(nblk):
            q = q_ref[j * blk:(j + 1) * blk, pl.ds(c0, HEAD_DIM)]
            kk = kbuf[j * blk:(j + 2) * blk, pl.ds(c0, HEAD_DIM)]
            vv = vbuf[j * blk:(j + 2) * blk, pl.ds(c0, HEAD_DIM)]
            s = lax.dot_general(q, kk, (((1,), (1,)), ((), ())), preferred_element_type=F32)
            s = s * scale + bias[min(j, 1)]
            m = jnp.max(s, axis=-1, keepdims=True)
            p = jnp.exp(s - m)
            den = jnp.sum(p, axis=-1, keepdims=True)
            o = jnp.dot(p.astype(BF16), vv, preferred_element_type=F32) / den
            o_ref[j * blk:(j + 1) * blk, pl.ds(c0, HEAD_DIM)] = o.astype(o_ref.dtype)
            parts.append(m + jnp.log(den))
        lse = jnp.concatenate(parts, axis=0) if nblk > 1 else parts[0]
        return jnp.where(lane == h, lse, lse_acc)

    lse_ref[...] = lax.fori_loop(0, n_heads, head, jnp.zeros((qb, LANES), F32))


def _attn_branch(q, k, v, *, qb=256):
    bsz, dil, length, d = q.shape
    n_heads = d // HEAD_DIM
    nq = length // qb
    per = qb // ATTN_BLOCK
    cur = pl.BlockSpec((None, None, qb, d), lambda b, r, i: (b, r, i, 0))
    prev = pl.BlockSpec((None, None, ATTN_BLOCK, d),
                        lambda b, r, i: (b, r, jnp.maximum(i * per - 1, 0), 0))
    kern = functools.partial(_attn_kernel, qb=qb, n_heads=n_heads)
    return pl.pallas_call(
        kern,
        grid=(bsz, dil, nq),
        in_specs=[cur, prev, cur, prev, cur],
        out_specs=[cur, pl.BlockSpec((None, None, qb, LANES), lambda b, r, i: (b, r, i, 0))],
        out_shape=[
            jax.ShapeDtypeStruct((bsz, dil, length, d), BF16),
            jax.ShapeDtypeStruct((bsz, dil, length, LANES), F32),
        ],
        scratch_shapes=[
            pltpu.VMEM((ATTN_BLOCK + qb, d), BF16),
            pltpu.VMEM((ATTN_BLOCK + qb, d), BF16),
            pltpu.VMEM((2, ATTN_BLOCK, 2 * ATTN_BLOCK), F32),
        ],
        compiler_params=pltpu.CompilerParams(
            dimension_semantics=("arbitrary", "arbitrary", "arbitrary"),
            vmem_limit_bytes=VMEM_LIMIT_BYTES),
        name=f"attn_d{dil}",
    )(q, k, k, v, v)


def _attn_out_kernel(o1_ref, o4_ref, o16_ref, l1_ref, l4_ref, l16_ref, h_ref, wo_ref,
                     g_ref, b_ref, out_ref, xbuf, *, rb, n_heads):
    d = wo_ref.shape[0]
    rows = xbuf.shape[0]
    for j in range(rb):
        cols = slice(j * d, (j + 1) * d)
        l1 = l1_ref[:, j * LANES:(j + 1) * LANES]
        l4 = l4_ref[j]
        l16 = l16_ref[j]
        mx = jnp.maximum(jnp.maximum(l1, l4), l16)
        e1 = jnp.exp(l1 - mx)
        e4 = jnp.exp(l4 - mx)
        e16 = jnp.exp(l16 - mx)
        tot = e1 + e4 + e16
        w1 = e1 / tot
        w4 = e4 / tot
        w16 = e16 / tot
        for h in range(n_heads):
            hc = slice(h * HEAD_DIM, (h + 1) * HEAD_DIM)
            hc1 = slice(j * d + h * HEAD_DIM, j * d + (h + 1) * HEAD_DIM)
            shape = (rows, HEAD_DIM)
            o = (jnp.broadcast_to(w1[:, h:h + 1], shape) * o1_ref[:, hc1].astype(F32)
                 + jnp.broadcast_to(w4[:, h:h + 1], shape) * o4_ref[j, :, hc].astype(F32)
                 + jnp.broadcast_to(w16[:, h:h + 1], shape) * o16_ref[j, :, hc].astype(F32))
            xbuf[:, hc] = o.astype(BF16)
        mix = jnp.dot(xbuf[...], wo_ref[...], preferred_element_type=F32)
        y = DEEPNORM_ALPHA * h_ref[:, cols] + mix
        out_ref[:, cols] = _layer_norm(y, g_ref[...], b_ref[...])


def _attn_out(o1, o4, o16, l1, l4, l16, h3d, w_o, ln_g, ln_b, *, rb=2):
    bsz, seq, d = h3d.shape
    rows = seq // MAX_DILATION
    n_heads = d // HEAD_DIM
    kern = functools.partial(_attn_out_kernel, rb=rb, n_heads=n_heads)
    out = pl.pallas_call(
        kern,
        grid=(bsz, MAX_DILATION // rb),
        in_specs=[
            _nat_spec(rows, d, rb), _d4_spec(rows, d, rb), _d16_spec(rows, d, rb),
            _nat_spec(rows, LANES, rb), _d4_spec(rows, LANES, rb), _d16_spec(rows, LANES, rb),
            _nat_spec(rows, d, rb),
            pl.BlockSpec((d, d), lambda b, j: (0, 0), pipeline_mode=pl.Buffered(1)),
            pl.BlockSpec((1, d), lambda b, j: (0, 0), pipeline_mode=pl.Buffered(1)),
            pl.BlockSpec((1, d), lambda b, j: (0, 0), pipeline_mode=pl.Buffered(1)),
        ],
        out_specs=_nat_spec(rows, d, rb),
        out_shape=jax.ShapeDtypeStruct((bsz, rows, MAX_DILATION * d), F32),
        scratch_shapes=[pltpu.VMEM((rows, d), BF16)],
        compiler_params=pltpu.CompilerParams(
            dimension_semantics=("arbitrary", "arbitrary"), vmem_limit_bytes=VMEM_LIMIT_BYTES),
        name="attn_out",
    )(_nat_view(o1, d), _d4_view(o4, d), o16,
      _nat_view(l1, LANES), _d4_view(l4, LANES), l16,
      _nat_view(h3d, d), w_o, ln_g, ln_b)
    return out.reshape(bsz, seq, d)


def _ffn_params(w_up, conv_w, conv_b, w_down, tf):
    d, two_f = w_up.shape
    f = two_f // 2
    fp = -(-f // tf) * tf
    pad = fp - f

    def cols(a):
        return jnp.pad(a, ((0, 0), (0, pad)))

    w_gate = cols(w_up[:, :f]).astype(BF16)
    w_val = cols(w_up[:, f:]).astype(BF16)
    w_dn = jnp.pad(w_down, ((0, pad), (0, 0))).astype(BF16)
    fill = jnp.zeros((SUBLANES - CONV_WIDTH - 1, two_f), F32)
    cp = jnp.concatenate([conv_w, conv_b[None, :], fill], axis=0)
    return w_gate, w_val, cols(cp[:, :f]), cols(cp[:, f:]), w_dn


def kernel(x, pool_w_in, pool_w_grp, pool_scale, pool_w_out, attn_w_q, attn_w_o, shared_w_k,
           shared_w_v, ffn_w_up, ffn_conv_w, ffn_conv_b, ffn_w_down, ln1_g, ln1_b, ln2_g, ln2_b):
    bsz, seq, d = x.shape
    n_a = pool_w_in.shape[0]
    n_layers = ffn_w_up.shape[0]
    tf = 2 * MXU_DIM
    row = lambda a: a.reshape(1, d)

    h = x.reshape(bsz * seq, d)
    kv = None
    for i in range(n_layers):
        if i < n_a:
            h = _pool_layer(h, pool_w_in[i].astype(BF16), pool_w_grp[i].astype(BF16),
                            row(pool_scale[i]), pool_w_out[i].astype(BF16),
                            row(ln1_g[i]), row(ln1_b[i]), seq=seq)
        else:
            h3d = h.reshape(bsz, seq, d)
            if kv is None:
                kv = (_proj(h3d, shared_w_k.astype(BF16)), _proj(h3d, shared_w_v.astype(BF16)))
            qs = _proj(h3d, attn_w_q[i - n_a].astype(BF16))
            outs = [_attn_branch(q, k, v) for q, k, v in zip(qs, kv[0], kv[1])]
            (o1, l1), (o4, l4), (o16, l16) = outs
            h = _attn_out(o1, o4, o16, l1, l4, l16, h3d, attn_w_o[i - n_a].astype(BF16),
                          row(ln1_g[i]), row(ln1_b[i])).reshape(bsz * seq, d)
        ffn = _ffn_params(ffn_w_up[i], ffn_conv_w[i], ffn_conv_b[i], ffn_w_down[i], tf)
        h = _ffn_layer(h, *ffn, row(ln2_g[i]), row(ln2_b[i]), seq=seq, tf=tf)
    return h.reshape(bsz, seq, d)
```

```python
import functools
import math

import jax
import jax.numpy as jnp
from jax import lax
from jax.experimental import pallas as pl
from jax.experimental.pallas import tpu as pltpu

POOL_WINDOWS = (2, 4, 8, 16)
HEAD_DIM = 128
DILATED_BRANCHES = ((128, 1), (512, 4), (2048, 16))
ATTN_BLOCK = 128
CONV_WIDTH = 3
DEPTH = 2
DEEPNORM_ALPHA = (2.0 * DEPTH) ** 0.25
LN_EPS = 1e-5
NEG_INF = -1e30

LANES = 128
SUBLANES = 8
MXU_DIM = 256
VMEM_LIMIT_BYTES = 56 * 1024 * 1024

DILATIONS = tuple(d for _, d in DILATED_BRANCHES)
POOL_HALO = max(POOL_WINDOWS)

F32 = jnp.float32
BF16 = jnp.bfloat16


def _layer_norm(y, g, b):
    mu = jnp.mean(y, axis=-1, keepdims=True)
    d = y - mu
    var = jnp.mean(d * d, axis=-1, keepdims=True)
    return d * lax.rsqrt(var + LN_EPS) * g + b


def _const_spec(shape):
    nd = len(shape)
    return pl.BlockSpec(shape, lambda *_: (0,) * nd, pipeline_mode=pl.Buffered(1))


def _pool_kernel(x_ref, win_ref, wgrp_ref, scale_ref, wout_ref, g_ref, b_ref, o_ref,
                 pbuf, mbuf, *, tm, blocks_per_seq):
    i = pl.program_id(0)
    blk = i % blocks_per_seq
    d_model = x_ref.shape[1]
    gdim = d_model // len(POOL_WINDOWS)

    @pl.when(blk == 0)
    def _():
        pbuf[0:POOL_HALO, :] = jnp.zeros((POOL_HALO, d_model), F32)

    @pl.when(blk != 0)
    def _():
        pbuf[0:POOL_HALO, :] = pbuf[tm:tm + POOL_HALO, :]

    x = x_ref[...]
    p = jnp.dot(x.astype(BF16), win_ref[...], preferred_element_type=F32)
    pbuf[POOL_HALO:POOL_HALO + tm, :] = p

    pos = blk * tm + lax.broadcasted_iota(jnp.int32, (tm, 1), 0)
    for g, w in enumerate(POOL_WINDOWS):
        cols = slice(g * gdim, (g + 1) * gdim)
        s = pbuf[:, cols]
        shift = 1
        while shift < w:
            s = s + pltpu.roll(s, shift, axis=0)
            shift *= 2
        cnt = jnp.minimum(pos + 1, w).astype(F32)
        pooled = s[POOL_HALO:, :] / cnt - p[:, cols]
        mixed = jnp.dot(pooled.astype(BF16), wgrp_ref[g], preferred_element_type=F32)
        mbuf[:, cols] = (mixed * scale_ref[:, cols]).astype(BF16)

    mix = jnp.dot(mbuf[...], wout_ref[...], preferred_element_type=F32)
    o_ref[...] = _layer_norm(DEEPNORM_ALPHA * x + mix, g_ref[...], b_ref[...])


def _pool_layer(x2d, w_in, w_grp, scale, w_out, ln_g, ln_b, *, seq, tm=256):
    t, d = x2d.shape
    ng, gdim, _ = w_grp.shape
    kern = functools.partial(_pool_kernel, tm=tm, blocks_per_seq=seq // tm)
    return pl.pallas_call(
        kern,
        grid=(t // tm,),
        in_specs=[
            pl.BlockSpec((tm, d), lambda i: (i, 0)),
            _const_spec((d, d)),
            _const_spec((ng, gdim, gdim)),
            _const_spec((1, d)),
            _const_spec((d, d)),
            _const_spec((1, d)),
            _const_spec((1, d)),
        ],
        out_specs=pl.BlockSpec((tm, d), lambda i: (i, 0)),
        out_shape=jax.ShapeDtypeStruct((t, d), F32),
        scratch_shapes=[
            pltpu.VMEM((POOL_HALO + tm, d), F32),
            pltpu.VMEM((tm, d), BF16),
        ],
        compiler_params=pltpu.CompilerParams(
            dimension_semantics=("arbitrary",), vmem_limit_bytes=VMEM_LIMIT_BYTES),
        name="pool_layer",
    )(x2d, w_in, w_grp, scale, w_out, ln_g, ln_b)


def _ffn_kernel(h_ref, wg_ref, wv_ref, cg_ref, cv_ref, wd_ref, g_ref, b_ref, o_ref,
                hb, acc, ug_buf, uv_buf, carry_g, carry_v, *, tm, blocks_per_seq):
    i = pl.program_id(0)
    f = pl.program_id(1)
    nf = pl.num_programs(1)
    seq_start = (i % blocks_per_seq) == 0
    tf = wg_ref.shape[1]

    @pl.when(f == 0)
    def _():
        hb[...] = h_ref[...].astype(BF16)

    ug = jnp.dot(hb[...], wg_ref[...], preferred_element_type=F32)
    uv = jnp.dot(hb[...], wv_ref[...], preferred_element_type=F32)

    @pl.when(seq_start)
    def _():
        ug_buf[0:SUBLANES, :] = jnp.zeros((SUBLANES, tf), F32)
        uv_buf[0:SUBLANES, :] = jnp.zeros((SUBLANES, tf), F32)

    @pl.when(jnp.logical_not(seq_start))
    def _():
        ug_buf[0:SUBLANES, :] = carry_g[f]
        uv_buf[0:SUBLANES, :] = carry_v[f]

    ug_buf[SUBLANES:SUBLANES + tm, :] = ug
    uv_buf[SUBLANES:SUBLANES + tm, :] = uv
    carry_g[f] = ug[tm - SUBLANES:tm, :]
    carry_v[f] = uv[tm - SUBLANES:tm, :]

    def conv(u, buf, cp_ref):
        c = cp_ref[CONV_WIDTH:CONV_WIDTH + 1, :] + u * cp_ref[CONV_WIDTH - 1:CONV_WIDTH, :]
        for lag in range(1, CONV_WIDTH):
            tap = CONV_WIDTH - 1 - lag
            c = c + buf[SUBLANES - lag:SUBLANES - lag + tm, :] * cp_ref[tap:tap + 1, :]
        return c

    gate = conv(ug, ug_buf, cg_ref)
    val = conv(uv, uv_buf, cv_ref)
    act = (gate / (1.0 + jnp.exp(-gate)) * val).astype(BF16)
    part = jnp.dot(act, wd_ref[...], preferred_element_type=F32)

    @pl.when(f == 0)
    def _():
        acc[...] = part

    @pl.when(f != 0)
    def _():
        acc[...] += part

    @pl.when(f == nf - 1)
    def _():
        y = DEEPNORM_ALPHA * h_ref[...] + acc[...]
        o_ref[...] = _layer_norm(y, g_ref[...], b_ref[...])


def _ffn_layer(h2d, w_gate, w_val, cp_gate, cp_val, w_down, ln_g, ln_b, *, seq, tm=512, tf=512):
    t, d = h2d.shape
    fp = w_gate.shape[1]
    nf = fp // tf
    kern = functools.partial(_ffn_kernel, tm=tm, blocks_per_seq=seq // tm)
    return pl.pallas_call(
        kern,
        grid=(t // tm, nf),
        in_specs=[
            pl.BlockSpec((tm, d), lambda i, f: (i, 0)),
            pl.BlockSpec((d, tf), lambda i, f: (0, f)),
            pl.BlockSpec((d, tf), lambda i, f: (0, f)),
            pl.BlockSpec((SUBLANES, tf), lambda i, f: (0, f)),
            pl.BlockSpec((SUBLANES, tf), lambda i, f: (0, f)),
            pl.BlockSpec((tf, d), lambda i, f: (f, 0)),
            _const_spec((1, d)),
            _const_spec((1, d)),
        ],
        out_specs=pl.BlockSpec((tm, d), lambda i, f: (i, 0)),
        out_shape=jax.ShapeDtypeStruct((t, d), F32),
        scratch_shapes=[
            pltpu.VMEM((tm, d), BF16),
            pltpu.VMEM((tm, d), F32),
            pltpu.VMEM((SUBLANES + tm, tf), F32),
            pltpu.VMEM((SUBLANES + tm, tf), F32),
            pltpu.VMEM((nf, SUBLANES, tf), F32),
            pltpu.VMEM((nf, SUBLANES, tf), F32),
        ],
        compiler_params=pltpu.CompilerParams(
            dimension_semantics=("arbitrary", "arbitrary"), vmem_limit_bytes=VMEM_LIMIT_BYTES),
        name="ffn_layer",
    )(h2d, w_gate, w_val, cp_gate, cp_val, w_down, ln_g, ln_b)


def _dilated_spec(dil, tm, width):
    return pl.BlockSpec((None, dil, tm // dil, width), lambda b, i: (b, 0, i, 0))


def _proj_kernel(x_ref, w_ref, *refs):
    out_refs, ybuf = refs[:-1], refs[-1]
    n_tiles, tm, _ = ybuf.shape
    y = jnp.dot(x_ref[...].astype(BF16), w_ref[...], preferred_element_type=F32)
    for c in range(n_tiles):
        ybuf[c] = y[:, c * LANES:(c + 1) * LANES]
    for dil, o_ref in zip(DILATIONS, out_refs):
        for c in range(n_tiles):
            for r in range(dil):
                o_ref[r, :, c * LANES:(c + 1) * LANES] = (
                    ybuf[c, pl.ds(r, tm // dil, stride=dil), :].astype(BF16))


def _proj(h3d, w, *, tm=512):
    bsz, seq, d = h3d.shape
    return pl.pallas_call(
        _proj_kernel,
        grid=(bsz, seq // tm),
        in_specs=[pl.BlockSpec((None, tm, d), lambda b, i: (b, i, 0)), _const_spec((d, d))],
        out_specs=[_dilated_spec(dil, tm, d) for dil in DILATIONS],
        out_shape=[jax.ShapeDtypeStruct((bsz, dil, seq // dil, d), BF16) for dil in DILATIONS],
        scratch_shapes=[pltpu.VMEM((d // LANES, tm, LANES), F32)],
        compiler_params=pltpu.CompilerParams(
            dimension_semantics=("arbitrary", "arbitrary"), vmem_limit_bytes=VMEM_LIMIT_BYTES),
        name="proj",
    )(h3d, w)


def _attn_kernel(q_ref, kp_ref, kc_ref, vp_ref, vc_ref, o_ref, lse_ref, bias,
                 *, qb, n_heads, heads_per_iter):
    i = pl.program_id(2)
    nblk = qb // ATTN_BLOCK
    blk = ATTN_BLOCK

    row = lax.broadcasted_iota(jnp.int32, (blk, 2 * blk), 0)
    col = lax.broadcasted_iota(jnp.int32, (blk, 2 * blk), 1)
    dist = blk + row - col
    band = (dist >= 0) & (dist <= blk)
    has_prev = jnp.broadcast_to(i > 0, band.shape)
    bias[1] = jnp.where(band, 0.0, NEG_INF)
    bias[0] = jnp.where(band & ((col >= blk) | has_prev), 0.0, NEG_INF)

    scale = 1.0 / math.sqrt(HEAD_DIM)
    log2e = math.log2(math.e)
    lane = lax.broadcasted_iota(jnp.int32, (blk, LANES), 1)
    ones = jnp.ones((2 * blk, HEAD_DIM), BF16)
    lse_ref[...] = jnp.zeros(lse_ref.shape, F32)

    def one_block(h, j):
        hc = pl.ds(pl.multiple_of(h * HEAD_DIM, HEAD_DIM), HEAD_DIM)
        rows = slice(j * blk, (j + 1) * blk)
        q = q_ref[rows, hc]
        if j == 0:
            kk = jnp.concatenate([kp_ref[:, hc], kc_ref[0:blk, hc]], axis=0)
            vv = jnp.concatenate([vp_ref[:, hc], vc_ref[0:blk, hc]], axis=0)
        else:
            kk = kc_ref[(j - 1) * blk:(j + 1) * blk, hc]
            vv = vc_ref[(j - 1) * blk:(j + 1) * blk, hc]
        s = lax.dot_general(q, kk, (((1,), (1,)), ((), ())), preferred_element_type=F32)
        s = s + bias[min(j, 1)]
        m = jnp.max(s, axis=-1, keepdims=True)
        p = jnp.exp2((s - m) * (scale * log2e))
        ov = jnp.dot(p.astype(BF16), jnp.concatenate([vv, ones], axis=1),
                     preferred_element_type=F32)
        den = ov[:, HEAD_DIM:]
        o_ref[rows, hc] = (ov[:, :HEAD_DIM] / den).astype(o_ref.dtype)
        lse = m * scale + jnp.log(den)
        lse_ref[rows, :] = jnp.where(lane == h, lse, lse_ref[rows, :])

    def head_group(g, carry):
        for hh in range(heads_per_iter):
            for j in range(nblk):
                one_block(g * heads_per_iter + hh, j)
        return carry

    lax.fori_loop(0, n_heads // heads_per_iter, head_group, 0)


def _attn_branch(q, k, v, *, qb, heads_per_iter):
    bsz, dil, length, d = q.shape
    n_heads = d // HEAD_DIM
    per = qb // ATTN_BLOCK
    cur = pl.BlockSpec((None, None, qb, d), lambda b, r, i: (b, r, i, 0))
    prev = pl.BlockSpec((None, None, ATTN_BLOCK, d),
                        lambda b, r, i: (b, r, jnp.maximum(i * per - 1, 0), 0))
    kern = functools.partial(_attn_kernel, qb=qb, n_heads=n_heads, heads_per_iter=heads_per_iter)
    return pl.pallas_call(
        kern,
        grid=(bsz, dil, length // qb),
        in_specs=[cur, prev, cur, prev, cur],
        out_specs=[cur, pl.BlockSpec((None, None, qb, LANES), lambda b, r, i: (b, r, i, 0))],
        out_shape=[
            jax.ShapeDtypeStruct((bsz, dil, length, d), BF16),
            jax.ShapeDtypeStruct((bsz, dil, length, LANES), F32),
        ],
        scratch_shapes=[pltpu.VMEM((2, ATTN_BLOCK, 2 * ATTN_BLOCK), F32)],
        compiler_params=pltpu.CompilerParams(
            dimension_semantics=("arbitrary", "arbitrary", "arbitrary"),
            vmem_limit_bytes=VMEM_LIMIT_BYTES),
        name=f"attn_d{dil}",
    )(q, k, k, v, v)


def _attn_out_kernel(*refs, n_heads):
    nb = len(DILATIONS)
    o_refs, l_refs = refs[:nb], refs[nb:2 * nb]
    h_ref, wo_ref, g_ref, b_ref, out_ref = refs[2 * nb:2 * nb + 5]
    o_nat, l_nat, xbuf = refs[2 * nb + 5:]
    tm = h_ref.shape[0]

    for k, dil in enumerate(DILATIONS):
        for r in range(dil):
            rows = pl.ds(r, tm // dil, stride=dil)
            l_nat[k, rows, :] = l_refs[k][r]
            for h in range(n_heads):
                hc = slice(h * HEAD_DIM, (h + 1) * HEAD_DIM)
                o_nat[k, h, rows, :] = o_refs[k][r, :, hc].astype(F32)

    lses = [l_nat[k] for k in range(nb)]
    mx = functools.reduce(jnp.maximum, lses)
    es = [jnp.exp(l - mx) for l in lses]
    tot = functools.reduce(lambda a, b: a + b, es)
    ws = [e / tot for e in es]
    for h in range(n_heads):
        o = None
        for k in range(nb):
            term = jnp.broadcast_to(ws[k][:, h:h + 1], (tm, HEAD_DIM)) * o_nat[k, h]
            o = term if o is None else o + term
        xbuf[:, h * HEAD_DIM:(h + 1) * HEAD_DIM] = o.astype(BF16)
    mix = jnp.dot(xbuf[...], wo_ref[...], preferred_element_type=F32)
    y = DEEPNORM_ALPHA * h_ref[...] + mix
    out_ref[...] = _layer_norm(y, g_ref[...], b_ref[...])


def _attn_out(os, ls, h3d, w_o, ln_g, ln_b, *, tm=256):
    bsz, seq, d = h3d.shape
    nb = len(DILATIONS)
    nat = pl.BlockSpec((None, tm, d), lambda b, i: (b, i, 0))
    kern = functools.partial(_attn_out_kernel, n_heads=d // HEAD_DIM)
    return pl.pallas_call(
        kern,
        grid=(bsz, seq // tm),
        in_specs=([_dilated_spec(dil, tm, d) for dil in DILATIONS]
                  + [_dilated_spec(dil, tm, LANES) for dil in DILATIONS]
                  + [nat, _const_spec((d, d)), _const_spec((1, d)), _const_spec((1, d))]),
        out_specs=nat,
        out_shape=jax.ShapeDtypeStruct((bsz, seq, d), F32),
        scratch_shapes=[
            pltpu.VMEM((nb, d // HEAD_DIM, tm, HEAD_DIM), F32),
            pltpu.VMEM((nb, tm, LANES), F32),
            pltpu.VMEM((tm, d), BF16),
        ],
        compiler_params=pltpu.CompilerParams(
            dimension_semantics=("arbitrary", "arbitrary"), vmem_limit_bytes=VMEM_LIMIT_BYTES),
        name="attn_out",
    )(*os, *ls, h3d, w_o, ln_g, ln_b)


def _ffn_params(w_up, conv_w, conv_b, w_down, tf):
    d, two_f = w_up.shape
    f = two_f // 2
    fp = -(-f // tf) * tf
    pad = fp - f

    def cols(a):
        return jnp.pad(a, ((0, 0), (0, pad)))

    w_gate = cols(w_up[:, :f]).astype(BF16)
    w_val = cols(w_up[:, f:]).astype(BF16)
    w_dn = jnp.pad(w_down, ((0, pad), (0, 0))).astype(BF16)
    fill = jnp.zeros((SUBLANES - CONV_WIDTH - 1, two_f), F32)
    cp = jnp.concatenate([conv_w, conv_b[None, :], fill], axis=0)
    return w_gate, w_val, cols(cp[:, :f]), cols(cp[:, f:]), w_dn


def kernel(x, pool_w_in, pool_w_grp, pool_scale, pool_w_out, attn_w_q, attn_w_o, shared_w_k,
           shared_w_v, ffn_w_up, ffn_conv_w, ffn_conv_b, ffn_w_down, ln1_g, ln1_b, ln2_g, ln2_b):
    bsz, seq, d = x.shape
    n_a = pool_w_in.shape[0]
    n_layers = ffn_w_up.shape[0]
    tf = 2 * MXU_DIM
    row = lambda a: a.reshape(1, d)

    h = x.reshape(bsz * seq, d)
    kv = None
    for i in range(n_layers):
        if i < n_a:
            h = _pool_layer(h, pool_w_in[i].astype(BF16), pool_w_grp[i].astype(BF16),
                            row(pool_scale[i]), pool_w_out[i].astype(BF16),
                            row(ln1_g[i]), row(ln1_b[i]), seq=seq)
        else:
            h3d = h.reshape(bsz, seq, d)
            if kv is None:
                kv = (_proj(h3d, shared_w_k.astype(BF16)), _proj(h3d, shared_w_v.astype(BF16)))
            qs = _proj(h3d, attn_w_q[i - n_a].astype(BF16))
            outs = []
            for q, k, v in zip(qs, kv[0], kv[1]):
                qb = min(q.shape[2], 4 * ATTN_BLOCK)
                outs.append(_attn_branch(q, k, v, qb=qb, heads_per_iter=8 * ATTN_BLOCK // qb))
            h = _attn_out([o for o, _ in outs], [l for _, l in outs], h3d,
                          attn_w_o[i - n_a].astype(BF16), row(ln1_g[i]), row(ln1_b[i]))
            h = h.reshape(bsz * seq, d)
        ffn = _ffn_params(ffn_w_up[i], ffn_conv_w[i], ffn_conv_b[i], ffn_w_down[i], tf)
        h = _ffn_layer(h, *ffn, row(ln2_g[i]), row(ln2_b[i]), seq=seq, tf=tf)
    return h.reshape(bsz, seq, d)
```

```python
import functools
import math

import jax
import jax.numpy as jnp
from jax import lax
from jax.experimental import pallas as pl
from jax.experimental.pallas import tpu as pltpu

POOL_WINDOWS = (2, 4, 8, 16)
HEAD_DIM = 128
DILATED_BRANCHES = ((128, 1), (512, 4), (2048, 16))
ATTN_BLOCK = 128
CONV_WIDTH = 3
DEPTH = 2
DEEPNORM_ALPHA = (2.0 * DEPTH) ** 0.25
LN_EPS = 1e-5
NEG_INF = -1e30

LANES = 128
SUBLANES = 8
MXU_DIM = 256
VMEM_LIMIT_BYTES = 56 * 1024 * 1024

DILATIONS = tuple(d for _, d in DILATED_BRANCHES)
POOL_HALO = max(POOL_WINDOWS)

F32 = jnp.float32
BF16 = jnp.bfloat16


def _layer_norm(y, g, b):
    mu = jnp.mean(y, axis=-1, keepdims=True)
    d = y - mu
    var = jnp.mean(d * d, axis=-1, keepdims=True)
    return d * lax.rsqrt(var + LN_EPS) * g + b


def _const_spec(shape):
    nd = len(shape)
    return pl.BlockSpec(shape, lambda *_: (0,) * nd, pipeline_mode=pl.Buffered(1))


def _pool_kernel(x_ref, win_ref, wgrp_ref, scale_ref, wout_ref, g_ref, b_ref, o_ref,
                 pbuf, mbuf, *, tm, blocks_per_seq):
    i = pl.program_id(0)
    blk = i % blocks_per_seq
    d_model = x_ref.shape[1]
    gdim = d_model // len(POOL_WINDOWS)

    @pl.when(blk == 0)
    def _():
        pbuf[0:POOL_HALO, :] = jnp.zeros((POOL_HALO, d_model), F32)

    @pl.when(blk != 0)
    def _():
        pbuf[0:POOL_HALO, :] = pbuf[tm:tm + POOL_HALO, :]

    x = x_ref[...]
    p = jnp.dot(x.astype(BF16), win_ref[...], preferred_element_type=F32)
    pbuf[POOL_HALO:POOL_HALO + tm, :] = p

    pos = blk * tm + lax.broadcasted_iota(jnp.int32, (tm, 1), 0)
    for g, w in enumerate(POOL_WINDOWS):
        cols = slice(g * gdim, (g + 1) * gdim)
        s = pbuf[:, cols]
        shift = 1
        while shift < w:
            s = s + pltpu.roll(s, shift, axis=0)
            shift *= 2
        cnt = jnp.minimum(pos + 1, w).astype(F32)
        pooled = s[POOL_HALO:, :] / cnt - p[:, cols]
        mixed = jnp.dot(pooled.astype(BF16), wgrp_ref[g], preferred_element_type=F32)
        mbuf[:, cols] = (mixed * scale_ref[:, cols]).astype(BF16)

    mix = jnp.dot(mbuf[...], wout_ref[...], preferred_element_type=F32)
    o_ref[...] = _layer_norm(DEEPNORM_ALPHA * x + mix, g_ref[...], b_ref[...])


def _pool_layer(x2d, w_in, w_grp, scale, w_out, ln_g, ln_b, *, seq, tm=256):
    t, d = x2d.shape
    ng, gdim, _ = w_grp.shape
    kern = functools.partial(_pool_kernel, tm=tm, blocks_per_seq=seq // tm)
    return pl.pallas_call(
        kern,
        grid=(t // tm,),
        in_specs=[
            pl.BlockSpec((tm, d), lambda i: (i, 0)),
            _const_spec((d, d)),
            _const_spec((ng, gdim, gdim)),
            _const_spec((1, d)),
            _const_spec((d, d)),
            _const_spec((1, d)),
            _const_spec((1, d)),
        ],
        out_specs=pl.BlockSpec((tm, d), lambda i: (i, 0)),
        out_shape=jax.ShapeDtypeStruct((t, d), F32),
        scratch_shapes=[
            pltpu.VMEM((POOL_HALO + tm, d), F32),
            pltpu.VMEM((tm, d), BF16),
        ],
        compiler_params=pltpu.CompilerParams(
            dimension_semantics=("arbitrary",), vmem_limit_bytes=VMEM_LIMIT_BYTES),
        name="pool_layer",
    )(x2d, w_in, w_grp, scale, w_out, ln_g, ln_b)


def _ffn_kernel(h_ref, wg_ref, wv_ref, cg_ref, cv_ref, wd_ref, g_ref, b_ref, o_ref,
                hb, acc, ug_buf, uv_buf, carry_g, carry_v, *act_bufs, tm, nf, blocks_per_seq):
    i = pl.program_id(0)
    f = pl.program_id(1)
    seq_start = (i % blocks_per_seq) == 0

    def conv(u, buf, carry, cp_ref, c):
        cs = slice(c * LANES, (c + 1) * LANES)
        uc = u[:, cs]
        buf[c, 0:SUBLANES, :] = jnp.where(seq_start, 0.0, carry[f, :, cs])
        buf[c, SUBLANES:SUBLANES + tm, :] = uc
        carry[f, :, cs] = uc[tm - SUBLANES:tm, :]
        out = cp_ref[CONV_WIDTH:CONV_WIDTH + 1, cs] + uc * cp_ref[CONV_WIDTH - 1:CONV_WIDTH, cs]
        for lag in range(1, CONV_WIDTH):
            tap = CONV_WIDTH - 1 - lag
            out = out + buf[c, SUBLANES - lag:SUBLANES - lag + tm, :] * cp_ref[tap:tap + 1, cs]
        return out

    def up_stage(act_dst):
        ug = jnp.dot(hb[...], wg_ref[...], preferred_element_type=F32)
        uv = jnp.dot(hb[...], wv_ref[...], preferred_element_type=F32)
        for c in range(ug.shape[1] // LANES):
            gate = conv(ug, ug_buf, carry_g, cg_ref, c)
            val = conv(uv, uv_buf, carry_v, cv_ref, c)
            act_dst[:, c * LANES:(c + 1) * LANES] = (
                gate / (1.0 + jnp.exp(-gate)) * val).astype(BF16)

    def down_stage(act_src):
        acc[...] += jnp.dot(act_src[...], wd_ref[...], preferred_element_type=F32)

    @pl.when(f == 0)
    def _():
        hb[...] = h_ref[...].astype(BF16)
        acc[...] = jnp.zeros(acc.shape, F32)
        up_stage(act_bufs[0])

    for parity in range(2):
        @pl.when((f > 0) & (f < nf) & (f % 2 == parity))
        def _():
            up_stage(act_bufs[parity])
            down_stage(act_bufs[1 - parity])

    @pl.when(f == nf)
    def _():
        down_stage(act_bufs[(nf - 1) % 2])
        y = DEEPNORM_ALPHA * h_ref[...] + acc[...]
        o_ref[...] = _layer_norm(y, g_ref[...], b_ref[...])


def _ffn_layer(h2d, w_gate, w_val, cp_gate, cp_val, w_down, ln_g, ln_b, *, seq, tm=512, tf=512):
    t, d = h2d.shape
    fp = w_gate.shape[1]
    nf = fp // tf
    kern = functools.partial(_ffn_kernel, tm=tm, nf=nf, blocks_per_seq=seq // tm)
    up_idx = lambda i, f: (0, jnp.minimum(f, nf - 1))
    return pl.pallas_call(
        kern,
        grid=(t // tm, nf + 1),
        in_specs=[
            pl.BlockSpec((tm, d), lambda i, f: (i, 0)),
            pl.BlockSpec((d, tf), up_idx),
            pl.BlockSpec((d, tf), up_idx),
            pl.BlockSpec((SUBLANES, tf), up_idx),
            pl.BlockSpec((SUBLANES, tf), up_idx),
            pl.BlockSpec((tf, d), lambda i, f: (jnp.maximum(f - 1, 0), 0)),
            _const_spec((1, d)),
            _const_spec((1, d)),
        ],
        out_specs=pl.BlockSpec((tm, d), lambda i, f: (i, 0)),
        out_shape=jax.ShapeDtypeStruct((t, d), F32),
        scratch_shapes=[
            pltpu.VMEM((tm, d), BF16),
            pltpu.VMEM((tm, d), F32),
            pltpu.VMEM((tf // LANES, SUBLANES + tm, LANES), F32),
            pltpu.VMEM((tf // LANES, SUBLANES + tm, LANES), F32),
            pltpu.VMEM((nf, SUBLANES, tf), F32),
            pltpu.VMEM((nf, SUBLANES, tf), F32),
            pltpu.VMEM((tm, tf), BF16),
            pltpu.VMEM((tm, tf), BF16),
        ],
        compiler_params=pltpu.CompilerParams(
            dimension_semantics=("arbitrary", "arbitrary"), vmem_limit_bytes=VMEM_LIMIT_BYTES),
        name="ffn_layer",
    )(h2d, w_gate, w_val, cp_gate, cp_val, w_down, ln_g, ln_b)


def _lagged_block(nblocks, blocks_per_seq, lag):
    def split(g):
        j = jnp.clip(g - lag, 0, nblocks - 1)
        return j // blocks_per_seq, j % blocks_per_seq
    return split


def _nat_spec(tm, width, split):
    def idx(g):
        b, i = split(g)
        return (b, i, 0)
    return pl.BlockSpec((None, tm, width), idx)


def _dilated_spec(dil, tm, width, split):
    def idx(g):
        b, i = split(g)
        return (b, 0, i, 0)
    return pl.BlockSpec((None, dil, tm // dil, width), idx)


def _proj_kernel(x_ref, w_ref, *refs, nblocks):
    nat_ref, o4_ref, o16_ref = refs[:3]
    ybufs, zbuf = refs[3:5], refs[5]
    n_tiles, tm, _ = ybufs[0].shape
    g = pl.program_id(0)
    tiles_per_dot = MXU_DIM // LANES

    def matmul(ybuf, xb, n):
        cols = slice(n * MXU_DIM, (n + 1) * MXU_DIM)
        y = jnp.dot(xb, w_ref[:, cols], preferred_element_type=F32)
        nat_ref[0, :, cols] = y.astype(BF16)
        for t in range(tiles_per_dot):
            ybuf[n * tiles_per_dot + t] = y[:, t * LANES:(t + 1) * LANES]

    def scatter(ybuf, n):
        for c in range(n * tiles_per_dot, (n + 1) * tiles_per_dot):
            cs = slice(c * LANES, (c + 1) * LANES)
            for r in range(4):
                slab = ybuf[c, pl.ds(r, tm // 4, stride=4), :]
                o4_ref[r, :, cs] = slab.astype(BF16)
                zbuf[c, r] = slab
            for r in range(4):
                for q in range(4):
                    o16_ref[4 * q + r, :, cs] = (
                        zbuf[c, r, pl.ds(q, tm // 16, stride=4), :].astype(BF16))

    def step(y_dst, y_src):
        xb = None if y_dst is None else x_ref[...].astype(BF16)
        for n in range(n_tiles // tiles_per_dot):
            if y_dst is not None:
                matmul(y_dst, xb, n)
            if y_src is not None:
                scatter(y_src, n)

    @pl.when(g == 0)
    def _():
        step(ybufs[0], None)

    for parity in range(2):
        @pl.when((g > 0) & (g < nblocks) & (g % 2 == parity))
        def _():
            step(ybufs[parity], ybufs[1 - parity])

    @pl.when(g == nblocks)
    def _():
        step(None, ybufs[(nblocks - 1) % 2])


def _proj(h3d, w, *, tm=512):
    bsz, seq, d = h3d.shape
    blocks_per_seq = seq // tm
    nblocks = bsz * blocks_per_seq
    cur = _lagged_block(nblocks, blocks_per_seq, 0)
    prev = _lagged_block(nblocks, blocks_per_seq, 1)
    assert DILATIONS == (1, 4, 16), "the two-step row de-interleave is written for 1, 4, 16"
    ybuf = pltpu.VMEM((d // LANES, tm, LANES), F32)
    zbuf = pltpu.VMEM((d // LANES, 4, tm // 4, LANES), F32)
    return pl.pallas_call(
        functools.partial(_proj_kernel, nblocks=nblocks),
        grid=(nblocks + 1,),
        in_specs=[_nat_spec(tm, d, cur), _const_spec((d, d))],
        out_specs=[_dilated_spec(1, tm, d, cur), _dilated_spec(4, tm, d, prev),
                   _dilated_spec(16, tm, d, prev)],
        out_shape=[jax.ShapeDtypeStruct((bsz, dil, seq // dil, d), BF16) for dil in DILATIONS],
        scratch_shapes=[ybuf, ybuf, zbuf],
        compiler_params=pltpu.CompilerParams(
            dimension_semantics=("arbitrary",), vmem_limit_bytes=VMEM_LIMIT_BYTES),
        name="proj",
    )(h3d, w)


def _attn_kernel(q_ref, kp_ref, kc_ref, vp_ref, vc_ref, o_ref, lse_ref, bias,
                 *, qb, n_heads, heads_per_iter):
    i = pl.program_id(2)
    nblk = qb // ATTN_BLOCK
    blk = ATTN_BLOCK

    row = lax.broadcasted_iota(jnp.int32, (blk, 2 * blk), 0)
    col = lax.broadcasted_iota(jnp.int32, (blk, 2 * blk), 1)
    dist = blk + row - col
    band = (dist >= 0) & (dist <= blk)
    has_prev = jnp.broadcast_to(i > 0, band.shape)
    bias[1] = jnp.where(band, 0.0, NEG_INF)
    bias[0] = jnp.where(band & ((col >= blk) | has_prev), 0.0, NEG_INF)

    scale = 1.0 / math.sqrt(HEAD_DIM)
    log2e = math.log2(math.e)
    lane = lax.broadcasted_iota(jnp.int32, (blk, LANES), 1)
    ones = jnp.ones((2 * blk, HEAD_DIM), BF16)
    lse_ref[...] = jnp.zeros(lse_ref.shape, F32)

    def one_block(h, j):
        hc = pl.ds(pl.multiple_of(h * HEAD_DIM, HEAD_DIM), HEAD_DIM)
        rows = slice(j * blk, (j + 1) * blk)
        q = q_ref[rows, hc]
        if j == 0:
            kk = jnp.concatenate([kp_ref[:, hc], kc_ref[0:blk, hc]], axis=0)
            vv = jnp.concatenate([vp_ref[:, hc], vc_ref[0:blk, hc]], axis=0)
        else:
            kk = kc_ref[(j - 1) * blk:(j + 1) * blk, hc]
            vv = vc_ref[(j - 1) * blk:(j + 1) * blk, hc]
        s = lax.dot_general(q, kk, (((1,), (1,)), ((), ())), preferred_element_type=F32)
        s = s + bias[min(j, 1)]
        m = jnp.max(s, axis=-1, keepdims=True)
        p = jnp.exp2((s - m) * (scale * log2e))
        ov = jnp.dot(p.astype(BF16), jnp.concatenate([vv, ones], axis=1),
                     preferred_element_type=F32)
        den = ov[:, HEAD_DIM:]
        o_ref[rows, hc] = (ov[:, :HEAD_DIM] / den).astype(o_ref.dtype)
        lse = m * scale + jnp.log(den)
        lse_ref[rows, :] = jnp.where(lane == h, lse, lse_ref[rows, :])

    def head_group(g, carry):
        for hh in range(heads_per_iter):
            for j in range(nblk):
                one_block(g * heads_per_iter + hh, j)
        return carry

    lax.fori_loop(0, n_heads // heads_per_iter, head_group, 0)


def _attn_branch(q, k, v, *, qb, heads_per_iter):
    bsz, dil, length, d = q.shape
    n_heads = d // HEAD_DIM
    per = qb // ATTN_BLOCK
    cur = pl.BlockSpec((None, None, qb, d), lambda b, r, i: (b, r, i, 0))
    prev = pl.BlockSpec((None, None, ATTN_BLOCK, d),
                        lambda b, r, i: (b, r, jnp.maximum(i * per - 1, 0), 0))
    kern = functools.partial(_attn_kernel, qb=qb, n_heads=n_heads, heads_per_iter=heads_per_iter)
    return pl.pallas_call(
        kern,
        grid=(bsz, dil, length // qb),
        in_specs=[cur, prev, cur, prev, cur],
        out_specs=[cur, pl.BlockSpec((None, None, qb, LANES), lambda b, r, i: (b, r, i, 0))],
        out_shape=[
            jax.ShapeDtypeStruct((bsz, dil, length, d), BF16),
            jax.ShapeDtypeStruct((bsz, dil, length, LANES), F32),
        ],
        scratch_shapes=[pltpu.VMEM((2, ATTN_BLOCK, 2 * ATTN_BLOCK), F32)],
        compiler_params=pltpu.CompilerParams(
            dimension_semantics=("arbitrary", "arbitrary", "arbitrary"),
            vmem_limit_bytes=VMEM_LIMIT_BYTES),
        name=f"attn_d{dil}",
    )(q, k, k, v, v)


def _attn_out_kernel(o1_ref, o4_ref, o16_ref, l1_ref, l4_ref, l16_ref, h_ref, wo_ref, g_ref,
                     b_ref, out_ref, o_nat, z16, l_nat, zl16, ybuf, *xbufs, n_heads, nblocks):
    tm = h_ref.shape[0]
    g = pl.program_id(0)
    heads_per_dot = MXU_DIM // HEAD_DIM

    def to_natural(dst, src4):
        for r in range(4):
            dst[pl.ds(r, tm // 4, stride=4), :] = src4(r)

    def from_d16(tmp, src16):
        for r in range(4):
            for q in range(4):
                tmp[r, pl.ds(q, tm // 16, stride=4), :] = src16(4 * q + r)
        return lambda r: tmp[r]

    def merge_weights():
        to_natural(l_nat.at[0], lambda r: l4_ref[r])
        to_natural(l_nat.at[1], from_d16(zl16, lambda r: l16_ref[r]))
        lses = [l1_ref[0], l_nat[0], l_nat[1]]
        mx = functools.reduce(jnp.maximum, lses)
        es = [jnp.exp(l - mx) for l in lses]
        tot = functools.reduce(lambda a, b: a + b, es)
        return es[0] / tot, es[1] / tot

    def merge_head(xbuf, w1, w4, h):
        hc = slice(h * HEAD_DIM, (h + 1) * HEAD_DIM)
        to_natural(o_nat.at[0, h], lambda r: o4_ref[r, :, hc].astype(F32))
        to_natural(o_nat.at[1, h],
                   from_d16(z16.at[h], lambda r: o16_ref[r, :, hc].astype(F32)))
        o1 = o1_ref[0, :, hc].astype(F32)
        o4 = o_nat[0, h]
        o16 = o_nat[1, h]
        shape = (tm, HEAD_DIM)
        o = (o16 + jnp.broadcast_to(w1[:, h:h + 1], shape) * (o1 - o16)
             + jnp.broadcast_to(w4[:, h:h + 1], shape) * (o4 - o16))
        xbuf[:, hc] = o.astype(BF16)

    def step(x_dst, x_src):
        if x_dst is not None:
            w1, w4 = merge_weights()
        for n in range(n_heads // heads_per_dot):
            if x_src is not None:
                cols = slice(n * MXU_DIM, (n + 1) * MXU_DIM)
                ybuf[:, cols] = jnp.dot(x_src[...], wo_ref[:, cols], preferred_element_type=F32)
            if x_dst is not None:
                for h in range(n * heads_per_dot, (n + 1) * heads_per_dot):
                    merge_head(x_dst, w1, w4, h)
        if x_src is not None:
            y = DEEPNORM_ALPHA * h_ref[...] + ybuf[...]
            out_ref[...] = _layer_norm(y, g_ref[...], b_ref[...])

    @pl.when(g == 0)
    def _():
        step(xbufs[0], None)

    for parity in range(2):
        @pl.when((g > 0) & (g < nblocks) & (g % 2 == parity))
        def _():
            step(xbufs[parity], xbufs[1 - parity])

    @pl.when(g == nblocks)
    def _():
        step(None, xbufs[(nblocks - 1) % 2])


def _attn_out(os, ls, h3d, w_o, ln_g, ln_b, *, tm=256):
    bsz, seq, d = h3d.shape
    assert DILATIONS == (1, 4, 16), "the two-step row interleave is written for 1, 4, 16"
    n_heads = d // HEAD_DIM
    blocks_per_seq = seq // tm
    nblocks = bsz * blocks_per_seq
    cur = _lagged_block(nblocks, blocks_per_seq, 0)
    prev = _lagged_block(nblocks, blocks_per_seq, 1)
    xbuf = pltpu.VMEM((tm, d), BF16)
    kern = functools.partial(_attn_out_kernel, n_heads=n_heads, nblocks=nblocks)
    return pl.pallas_call(
        kern,
        grid=(nblocks + 1,),
        in_specs=([_dilated_spec(dil, tm, d, cur) for dil in DILATIONS]
                  + [_dilated_spec(dil, tm, LANES, cur) for dil in DILATIONS]
                  + [_nat_spec(tm, d, prev), _const_spec((d, d)), _const_spec((1, d)),
                     _const_spec((1, d))]),
        out_specs=_nat_spec(tm, d, prev),
        out_shape=jax.ShapeDtypeStruct((bsz, seq, d), F32),
        scratch_shapes=[
            pltpu.VMEM((2, n_heads, tm, HEAD_DIM), F32),
            pltpu.VMEM((n_heads, 4, tm // 4, HEAD_DIM), F32),
            pltpu.VMEM((2, tm, LANES), F32),
            pltpu.VMEM((4, tm // 4, LANES), F32),
            pltpu.VMEM((tm, d), F32),
            xbuf, xbuf,
        ],
        compiler_params=pltpu.CompilerParams(
            dimension_semantics=("arbitrary",), vmem_limit_bytes=VMEM_LIMIT_BYTES),
        name="attn_out",
    )(*os, *ls, h3d, w_o, ln_g, ln_b)


def _ffn_params(w_up, conv_w, conv_b, w_down, tf):
    d, two_f = w_up.shape
    f = two_f // 2
    fp = -(-f // tf) * tf
    pad = fp - f

    def cols(a):
        return jnp.pad(a, ((0, 0), (0, pad)))

    w_gate = cols(w_up[:, :f]).astype(BF16)
    w_val = cols(w_up[:, f:]).astype(BF16)
    w_dn = jnp.pad(w_down, ((0, pad), (0, 0))).astype(BF16)
    fill = jnp.zeros((SUBLANES - CONV_WIDTH - 1, two_f), F32)
    cp = jnp.concatenate([conv_w, conv_b[None, :], fill], axis=0)
    return w_gate, w_val, cols(cp[:, :f]), cols(cp[:, f:]), w_dn


def kernel(x, pool_w_in, pool_w_grp, pool_scale, pool_w_out, attn_w_q, attn_w_o, shared_w_k,
           shared_w_v, ffn_w_up, ffn_conv_w, ffn_conv_b, ffn_w_down, ln1_g, ln1_b, ln2_g, ln2_b):
    bsz, seq, d = x.shape
    n_a = pool_w_in.shape[0]
    n_layers = ffn_w_up.shape[0]
    tf = 2 * MXU_DIM
    row = lambda a: a.reshape(1, d)

    h = x.reshape(bsz * seq, d)
    kv = None
    for i in range(n_layers):
        if i < n_a:
            h = _pool_layer(h, pool_w_in[i].astype(BF16), pool_w_grp[i].astype(BF16),
                            row(pool_scale[i]), pool_w_out[i].astype(BF16),
                            row(ln1_g[i]), row(ln1_b[i]), seq=seq)
        else:
            h3d = h.reshape(bsz, seq, d)
            if kv is None:
                kv = (_proj(h3d, shared_w_k.astype(BF16)), _proj(h3d, shared_w_v.astype(BF16)))
            qs = _proj(h3d, attn_w_q[i - n_a].astype(BF16))
            outs = []
            for q, k, v in zip(qs, kv[0], kv[1]):
                qb = min(q.shape[2], 4 * ATTN_BLOCK)
                outs.append(_attn_branch(q, k, v, qb=qb, heads_per_iter=8 * ATTN_BLOCK // qb))
            h = _attn_out([o for o, _ in outs], [l for _, l in outs], h3d,
                          attn_w_o[i - n_a].astype(BF16), row(ln1_g[i]), row(ln1_b[i]))
            h = h.reshape(bsz * seq, d)
        ffn = _ffn_params(ffn_w_up[i], ffn_conv_w[i], ffn_conv_b[i], ffn_w_down[i], tf)
        h = _ffn_layer(h, *ffn, row(ln2_g[i]), row(ln2_b[i]), seq=seq, tf=tf)
    return h.reshape(bsz, seq, d)
```

```python
import functools
import math

import jax
import jax.numpy as jnp
from jax import lax
from jax.experimental import pallas as pl
from jax.experimental.pallas import tpu as pltpu

POOL_WINDOWS = (2, 4, 8, 16)
HEAD_DIM = 128
DILATED_BRANCHES = ((128, 1), (512, 4), (2048, 16))
ATTN_BLOCK = 128
CONV_WIDTH = 3
DEPTH = 2
DEEPNORM_ALPHA = (2.0 * DEPTH) ** 0.25
LN_EPS = 1e-5
NEG_INF = -1e30

LANES = 128
SUBLANES = 8
MXU_DIM = 256
VMEM_LIMIT_BYTES = 56 * 1024 * 1024

DILATIONS = tuple(d for _, d in DILATED_BRANCHES)
POOL_HALO = max(POOL_WINDOWS)

F32 = jnp.float32
BF16 = jnp.bfloat16


def _layer_norm(y, g, b):
    mu = jnp.mean(y, axis=-1, keepdims=True)
    d = y - mu
    var = jnp.mean(d * d, axis=-1, keepdims=True)
    return d * lax.rsqrt(var + LN_EPS) * g + b


def _const_spec(shape):
    nd = len(shape)
    return pl.BlockSpec(shape, lambda *_: (0,) * nd, pipeline_mode=pl.Buffered(1))


def _pool_kernel(x_ref, win_ref, wgrp_ref, scale_ref, wout_ref, g_ref, b_ref, o_ref,
                 pbuf, mbuf, *, tm, blocks_per_seq):
    i = pl.program_id(0)
    blk = i % blocks_per_seq
    d_model = x_ref.shape[1]
    gdim = d_model // len(POOL_WINDOWS)

    @pl.when(blk == 0)
    def _():
        pbuf[0:POOL_HALO, :] = jnp.zeros((POOL_HALO, d_model), F32)

    @pl.when(blk != 0)
    def _():
        pbuf[0:POOL_HALO, :] = pbuf[tm:tm + POOL_HALO, :]

    x = x_ref[...]
    p = jnp.dot(x.astype(BF16), win_ref[...], preferred_element_type=F32)
    pbuf[POOL_HALO:POOL_HALO + tm, :] = p

    pos = blk * tm + lax.broadcasted_iota(jnp.int32, (tm, 1), 0)
    for g, w in enumerate(POOL_WINDOWS):
        cols = slice(g * gdim, (g + 1) * gdim)
        s = pbuf[:, cols]
        shift = 1
        while shift < w:
            s = s + pltpu.roll(s, shift, axis=0)
            shift *= 2
        cnt = jnp.minimum(pos + 1, w).astype(F32)
        pooled = s[POOL_HALO:, :] / cnt - p[:, cols]
        mixed = jnp.dot(pooled.astype(BF16), wgrp_ref[g], preferred_element_type=F32)
        mbuf[:, cols] = (mixed * scale_ref[:, cols]).astype(BF16)

    mix = jnp.dot(mbuf[...], wout_ref[...], preferred_element_type=F32)
    o_ref[...] = _layer_norm(DEEPNORM_ALPHA * x + mix, g_ref[...], b_ref[...])


def _pool_layer(x2d, w_in, w_grp, scale, w_out, ln_g, ln_b, *, seq, tm=256):
    t, d = x2d.shape
    ng, gdim, _ = w_grp.shape
    kern = functools.partial(_pool_kernel, tm=tm, blocks_per_seq=seq // tm)
    return pl.pallas_call(
        kern,
        grid=(t // tm,),
        in_specs=[
            pl.BlockSpec((tm, d), lambda i: (i, 0)),
            _const_spec((d, d)),
            _const_spec((ng, gdim, gdim)),
            _const_spec((1, d)),
            _const_spec((d, d)),
            _const_spec((1, d)),
            _const_spec((1, d)),
        ],
        out_specs=pl.BlockSpec((tm, d), lambda i: (i, 0)),
        out_shape=jax.ShapeDtypeStruct((t, d), F32),
        scratch_shapes=[
            pltpu.VMEM((POOL_HALO + tm, d), F32),
            pltpu.VMEM((tm, d), BF16),
        ],
        compiler_params=pltpu.CompilerParams(
            dimension_semantics=("arbitrary",), vmem_limit_bytes=VMEM_LIMIT_BYTES),
        name="pool_layer",
    )(x2d, w_in, w_grp, scale, w_out, ln_g, ln_b)


def _ffn_kernel(h_ref, wg_ref, wv_ref, cp_ref, wd_ref, g_ref, b_ref, o_ref,
                hb, acc, ug_buf, uv_buf, carry_g, carry_v, *act_bufs, tm, nf, blocks_per_seq):
    i = pl.program_id(0)
    f = pl.program_id(1)
    seq_start = (i % blocks_per_seq) == 0
    tf = wg_ref.shape[1]
    fp = nf * tf

    def conv(uc, buf, carry, cp_base, c):
        cs = slice(c * LANES, (c + 1) * LANES)
        cp = cp_ref[:, pl.ds(pl.multiple_of(cp_base + f * tf + c * LANES, LANES), LANES)]
        buf[c, 0:SUBLANES, :] = jnp.where(seq_start, 0.0, carry[f, :, cs])
        buf[c, SUBLANES:SUBLANES + tm, :] = uc
        carry[f, :, cs] = uc[tm - SUBLANES:tm, :]
        out = cp[CONV_WIDTH:CONV_WIDTH + 1, :] + uc * cp[CONV_WIDTH - 1:CONV_WIDTH, :]
        for lag in range(1, CONV_WIDTH):
            tap = CONV_WIDTH - 1 - lag
            out = out + buf[c, SUBLANES - lag:SUBLANES - lag + tm, :] * cp[tap:tap + 1, :]
        return out

    def up_stage(act_dst):
        ug = jnp.dot(hb[...], wg_ref[...], preferred_element_type=F32)
        uv = jnp.dot(hb[...], wv_ref[...], preferred_element_type=F32)
        for c in range(tf // LANES):
            cs = slice(c * LANES, (c + 1) * LANES)
            gate = conv(ug[:, cs], ug_buf, carry_g, 0, c)
            val = conv(uv[:, cs], uv_buf, carry_v, fp, c)
            act_dst[:, cs] = (gate / (1.0 + jnp.exp(-gate)) * val).astype(BF16)

    def down_stage(act_src):
        acc[...] += jnp.dot(act_src[...], wd_ref[...], preferred_element_type=F32)

    @pl.when(f == 0)
    def _():
        hb[...] = h_ref[...].astype(BF16)
        acc[...] = jnp.zeros(acc.shape, F32)
        up_stage(act_bufs[0])

    for parity in range(2):
        @pl.when((f > 0) & (f < nf) & (f % 2 == parity))
        def _():
            up_stage(act_bufs[parity])
            down_stage(act_bufs[1 - parity])

    @pl.when(f == nf)
    def _():
        down_stage(act_bufs[(nf - 1) % 2])
        y = DEEPNORM_ALPHA * h_ref[...] + acc[...]
        o_ref[...] = _layer_norm(y, g_ref[...], b_ref[...])


def _ffn_layer(h2d, w_gv, cp, w_down, ln_g, ln_b, *, seq, tm=512, tf=512):
    t, d = h2d.shape
    fp = w_down.shape[0]
    nf = fp // tf
    kern = functools.partial(_ffn_kernel, tm=tm, nf=nf, blocks_per_seq=seq // tm)
    return pl.pallas_call(
        kern,
        grid=(t // tm, nf + 1),
        in_specs=[
            pl.BlockSpec((tm, d), lambda i, f: (i, 0)),
            pl.BlockSpec((d, tf), lambda i, f: (0, jnp.minimum(f, nf - 1))),
            pl.BlockSpec((d, tf), lambda i, f: (0, nf + jnp.minimum(f, nf - 1))),
            _const_spec((SUBLANES, 2 * fp)),
            pl.BlockSpec((tf, d), lambda i, f: (jnp.maximum(f - 1, 0), 0)),
            _const_spec((1, d)),
            _const_spec((1, d)),
        ],
        out_specs=pl.BlockSpec((tm, d), lambda i, f: (i, 0)),
        out_shape=jax.ShapeDtypeStruct((t, d), F32),
        scratch_shapes=[
            pltpu.VMEM((tm, d), BF16),
            pltpu.VMEM((tm, d), F32),
            pltpu.VMEM((tf // LANES, SUBLANES + tm, LANES), F32),
            pltpu.VMEM((tf // LANES, SUBLANES + tm, LANES), F32),
            pltpu.VMEM((nf, SUBLANES, tf), F32),
            pltpu.VMEM((nf, SUBLANES, tf), F32),
            pltpu.VMEM((tm, tf), BF16),
            pltpu.VMEM((tm, tf), BF16),
        ],
        compiler_params=pltpu.CompilerParams(
            dimension_semantics=("arbitrary", "arbitrary"), vmem_limit_bytes=VMEM_LIMIT_BYTES),
        name="ffn_layer",
    )(h2d, w_gv, w_gv, cp, w_down, ln_g, ln_b)


def _lagged_block(nblocks, blocks_per_seq, lag):
    def split(g):
        j = jnp.clip(g - lag, 0, nblocks - 1)
        return j // blocks_per_seq, j % blocks_per_seq
    return split


def _nat_spec(tm, width, split):
    def idx(g):
        b, i = split(g)
        return (b, i, 0)
    return pl.BlockSpec((None, tm, width), idx)


def _dilated_spec(dil, tm, width, split):
    def idx(g):
        b, i = split(g)
        return (b, 0, i, 0)
    return pl.BlockSpec((None, dil, tm // dil, width), idx)


def _proj_kernel(x_ref, w_ref, *refs, nblocks):
    nat_ref, o4_ref, o16_ref = refs[:3]
    ybufs, zbuf = refs[3:5], refs[5]
    n_tiles, tm, _ = ybufs[0].shape
    g = pl.program_id(0)
    tiles_per_dot = MXU_DIM // LANES

    def matmul(ybuf, xb, n):
        cols = slice(n * MXU_DIM, (n + 1) * MXU_DIM)
        y = jnp.dot(xb, w_ref[:, cols], preferred_element_type=F32)
        nat_ref[0, :, cols] = y.astype(BF16)
        for t in range(tiles_per_dot):
            ybuf[n * tiles_per_dot + t] = y[:, t * LANES:(t + 1) * LANES]

    def scatter(ybuf, n):
        for c in range(n * tiles_per_dot, (n + 1) * tiles_per_dot):
            cs = slice(c * LANES, (c + 1) * LANES)
            for r in range(4):
                slab = ybuf[c, pl.ds(r, tm // 4, stride=4), :]
                o4_ref[r, :, cs] = slab.astype(BF16)
                zbuf[c, r] = slab
            for r in range(4):
                for q in range(4):
                    o16_ref[4 * q + r, :, cs] = (
                        zbuf[c, r, pl.ds(q, tm // 16, stride=4), :].astype(BF16))

    def step(y_dst, y_src):
        xb = None if y_dst is None else x_ref[...].astype(BF16)
        for n in range(n_tiles // tiles_per_dot):
            if y_dst is not None:
                matmul(y_dst, xb, n)
            if y_src is not None:
                scatter(y_src, n)

    @pl.when(g == 0)
    def _():
        step(ybufs[0], None)

    for parity in range(2):
        @pl.when((g > 0) & (g < nblocks) & (g % 2 == parity))
        def _():
            step(ybufs[parity], ybufs[1 - parity])

    @pl.when(g == nblocks)
    def _():
        step(None, ybufs[(nblocks - 1) % 2])


def _proj(h3d, w, *, tm=512):
    bsz, seq, d = h3d.shape
    blocks_per_seq = seq // tm
    nblocks = bsz * blocks_per_seq
    cur = _lagged_block(nblocks, blocks_per_seq, 0)
    prev = _lagged_block(nblocks, blocks_per_seq, 1)
    assert DILATIONS == (1, 4, 16), "the two-step row de-interleave is written for 1, 4, 16"
    ybuf = pltpu.VMEM((d // LANES, tm, LANES), F32)
    zbuf = pltpu.VMEM((d // LANES, 4, tm // 4, LANES), F32)
    return pl.pallas_call(
        functools.partial(_proj_kernel, nblocks=nblocks),
        grid=(nblocks + 1,),
        in_specs=[_nat_spec(tm, d, cur), _const_spec((d, d))],
        out_specs=[_dilated_spec(1, tm, d, cur), _dilated_spec(4, tm, d, prev),
                   _dilated_spec(16, tm, d, prev)],
        out_shape=[jax.ShapeDtypeStruct((bsz, dil, seq // dil, d), BF16) for dil in DILATIONS],
        scratch_shapes=[ybuf, ybuf, zbuf],
        compiler_params=pltpu.CompilerParams(
            dimension_semantics=("arbitrary",), vmem_limit_bytes=VMEM_LIMIT_BYTES),
        name="proj",
    )(h3d, w)


def _attn_kernel(q_ref, kp_ref, kc_ref, vp_ref, vc_ref, o_ref, lse_ref, bias,
                 *, qb, n_heads, heads_per_iter):
    i = pl.program_id(2)
    nblk = qb // ATTN_BLOCK
    blk = ATTN_BLOCK

    row = lax.broadcasted_iota(jnp.int32, (blk, 2 * blk), 0)
    col = lax.broadcasted_iota(jnp.int32, (blk, 2 * blk), 1)
    dist = blk + row - col
    band = (dist >= 0) & (dist <= blk)
    has_prev = jnp.broadcast_to(i > 0, band.shape)
    bias[1] = jnp.where(band, 0.0, NEG_INF)
    bias[0] = jnp.where(band & ((col >= blk) | has_prev), 0.0, NEG_INF)

    scale = 1.0 / math.sqrt(HEAD_DIM)
    log2e = math.log2(math.e)
    lane = lax.broadcasted_iota(jnp.int32, (blk, LANES), 1)
    ones = jnp.ones((2 * blk, HEAD_DIM), BF16)
    lse_ref[...] = jnp.zeros(lse_ref.shape, F32)

    def one_block(h, j):
        hc = pl.ds(pl.multiple_of(h * HEAD_DIM, HEAD_DIM), HEAD_DIM)
        rows = slice(j * blk, (j + 1) * blk)
        q = q_ref[rows, hc]
        if j == 0:
            kk = jnp.concatenate([kp_ref[:, hc], kc_ref[0:blk, hc]], axis=0)
            vv = jnp.concatenate([vp_ref[:, hc], vc_ref[0:blk, hc]], axis=0)
        else:
            kk = kc_ref[(j - 1) * blk:(j + 1) * blk, hc]
            vv = vc_ref[(j - 1) * blk:(j + 1) * blk, hc]
        s = lax.dot_general(q, kk, (((1,), (1,)), ((), ())), preferred_element_type=F32)
        s = s + bias[min(j, 1)]
        m = jnp.max(s, axis=-1, keepdims=True)
        p = jnp.exp2((s - m) * (scale * log2e))
        ov = jnp.dot(p.astype(BF16), jnp.concatenate([vv, ones], axis=1),
                     preferred_element_type=F32)
        den = ov[:, HEAD_DIM:]
        o_ref[rows, hc] = (ov[:, :HEAD_DIM] / den).astype(o_ref.dtype)
        lse = m * scale + jnp.log(den)
        lse_ref[rows, :] = jnp.where(lane == h, lse, lse_ref[rows, :])

    def head_group(g, carry):
        for hh in range(heads_per_iter):
            for j in range(nblk):
                one_block(g * heads_per_iter + hh, j)
        return carry

    lax.fori_loop(0, n_heads // heads_per_iter, head_group, 0)


def _attn_branch(q, k, v, *, qb, heads_per_iter):
    bsz, dil, length, d = q.shape
    n_heads = d // HEAD_DIM
    per = qb // ATTN_BLOCK
    cur = pl.BlockSpec((None, None, qb, d), lambda b, r, i: (b, r, i, 0))
    prev = pl.BlockSpec((None, None, ATTN_BLOCK, d),
                        lambda b, r, i: (b, r, jnp.maximum(i * per - 1, 0), 0))
    kern = functools.partial(_attn_kernel, qb=qb, n_heads=n_heads, heads_per_iter=heads_per_iter)
    return pl.pallas_call(
        kern,
        grid=(bsz, dil, length // qb),
        in_specs=[cur, prev, cur, prev, cur],
        out_specs=[cur, pl.BlockSpec((None, None, qb, LANES), lambda b, r, i: (b, r, i, 0))],
        out_shape=[
            jax.ShapeDtypeStruct((bsz, dil, length, d), BF16),
            jax.ShapeDtypeStruct((bsz, dil, length, LANES), F32),
        ],
        scratch_shapes=[pltpu.VMEM((2, ATTN_BLOCK, 2 * ATTN_BLOCK), F32)],
        compiler_params=pltpu.CompilerParams(
            dimension_semantics=("arbitrary", "arbitrary", "arbitrary"),
            vmem_limit_bytes=VMEM_LIMIT_BYTES),
        name=f"attn_d{dil}",
    )(q, k, k, v, v)


def _attn_out_kernel(o1_ref, o4_ref, o16_ref, l1_ref, l4_ref, l16_ref, h_ref, wo_ref, g_ref,
                     b_ref, out_ref, o_nat, z16, l_nat, zl16, ybuf, *xbufs, n_heads, nblocks):
    tm = h_ref.shape[0]
    g = pl.program_id(0)
    heads_per_dot = MXU_DIM // HEAD_DIM

    def to_natural(dst, src4):
        for r in range(4):
            dst[pl.ds(r, tm // 4, stride=4), :] = src4(r)

    def from_d16(tmp, src16):
        for r in range(4):
            for q in range(4):
                tmp[r, pl.ds(q, tm // 16, stride=4), :] = src16(4 * q + r)
        return lambda r: tmp[r]

    def merge_weights():
        to_natural(l_nat.at[0], lambda r: l4_ref[r])
        to_natural(l_nat.at[1], from_d16(zl16, lambda r: l16_ref[r]))
        lses = [l1_ref[0], l_nat[0], l_nat[1]]
        mx = functools.reduce(jnp.maximum, lses)
        es = [jnp.exp(l - mx) for l in lses]
        tot = functools.reduce(lambda a, b: a + b, es)
        return es[0] / tot, es[1] / tot

    def merge_head(xbuf, w1, w4, h):
        hc = slice(h * HEAD_DIM, (h + 1) * HEAD_DIM)
        to_natural(o_nat.at[0, h], lambda r: o4_ref[r, :, hc].astype(F32))
        to_natural(o_nat.at[1, h],
                   from_d16(z16.at[h], lambda r: o16_ref[r, :, hc].astype(F32)))
        o1 = o1_ref[0, :, hc].astype(F32)
        o4 = o_nat[0, h]
        o16 = o_nat[1, h]
        shape = (tm, HEAD_DIM)
        o = (o16 + jnp.broadcast_to(w1[:, h:h + 1], shape) * (o1 - o16)
             + jnp.broadcast_to(w4[:, h:h + 1], shape) * (o4 - o16))
        xbuf[:, hc] = o.astype(BF16)

    def step(x_dst, x_src):
        if x_dst is not None:
            w1, w4 = merge_weights()
        for n in range(n_heads // heads_per_dot):
            if x_src is not None:
                cols = slice(n * MXU_DIM, (n + 1) * MXU_DIM)
                ybuf[:, cols] = jnp.dot(x_src[...], wo_ref[:, cols], preferred_element_type=F32)
            if x_dst is not None:
                for h in range(n * heads_per_dot, (n + 1) * heads_per_dot):
                    merge_head(x_dst, w1, w4, h)
        if x_src is not None:
            y = DEEPNORM_ALPHA * h_ref[...] + ybuf[...]
            out_ref[...] = _layer_norm(y, g_ref[...], b_ref[...])

    @pl.when(g == 0)
    def _():
        step(xbufs[0], None)

    for parity in range(2):
        @pl.when((g > 0) & (g < nblocks) & (g % 2 == parity))
        def _():
            step(xbufs[parity], xbufs[1 - parity])

    @pl.when(g == nblocks)
    def _():
        step(None, xbufs[(nblocks - 1) % 2])


def _attn_out(os, ls, h3d, w_o, ln_g, ln_b, *, tm=256):
    bsz, seq, d = h3d.shape
    assert DILATIONS == (1, 4, 16), "the two-step row interleave is written for 1, 4, 16"
    n_heads = d // HEAD_DIM
    blocks_per_seq = seq // tm
    nblocks = bsz * blocks_per_seq
    cur = _lagged_block(nblocks, blocks_per_seq, 0)
    prev = _lagged_block(nblocks, blocks_per_seq, 1)
    xbuf = pltpu.VMEM((tm, d), BF16)
    kern = functools.partial(_attn_out_kernel, n_heads=n_heads, nblocks=nblocks)
    return pl.pallas_call(
        kern,
        grid=(nblocks + 1,),
        in_specs=([_dilated_spec(dil, tm, d, cur) for dil in DILATIONS]
                  + [_dilated_spec(dil, tm, LANES, cur) for dil in DILATIONS]
                  + [_nat_spec(tm, d, prev), _const_spec((d, d)), _const_spec((1, d)),
                     _const_spec((1, d))]),
        out_specs=_nat_spec(tm, d, prev),
        out_shape=jax.ShapeDtypeStruct((bsz, seq, d), F32),
        scratch_shapes=[
            pltpu.VMEM((2, n_heads, tm, HEAD_DIM), F32),
            pltpu.VMEM((n_heads, 4, tm // 4, HEAD_DIM), F32),
            pltpu.VMEM((2, tm, LANES), F32),
            pltpu.VMEM((4, tm // 4, LANES), F32),
            pltpu.VMEM((tm, d), F32),
            xbuf, xbuf,
        ],
        compiler_params=pltpu.CompilerParams(
            dimension_semantics=("arbitrary",), vmem_limit_bytes=VMEM_LIMIT_BYTES),
        name="attn_out",
    )(*os, *ls, h3d, w_o, ln_g, ln_b)


def _ffn_params(w_up, conv_w, conv_b, w_down, tf):
    d, two_f = w_up.shape
    f = two_f // 2
    pad = -f % tf

    def gate_val(a, dtype):
        z = jnp.zeros((a.shape[0], pad), dtype)
        return jnp.concatenate([a[:, :f].astype(dtype), z, a[:, f:].astype(dtype), z], axis=1)

    w_gv = gate_val(w_up, BF16)
    w_dn = jnp.concatenate([w_down.astype(BF16), jnp.zeros((pad, d), BF16)], axis=0)
    fill = jnp.zeros((SUBLANES - CONV_WIDTH - 1, two_f), F32)
    cp = gate_val(jnp.concatenate([conv_w, conv_b[None, :], fill], axis=0), F32)
    return w_gv, cp, w_dn


def kernel(x, pool_w_in, pool_w_grp, pool_scale, pool_w_out, attn_w_q, attn_w_o, shared_w_k,
           shared_w_v, ffn_w_up, ffn_conv_w, ffn_conv_b, ffn_w_down, ln1_g, ln1_b, ln2_g, ln2_b):
    bsz, seq, d = x.shape
    n_a = pool_w_in.shape[0]
    n_layers = ffn_w_up.shape[0]
    tf = 2 * MXU_DIM
    row = lambda a: a.reshape(1, d)

    h = x.reshape(bsz * seq, d)
    kv = None
    for i in range(n_layers):
        if i < n_a:
            h = _pool_layer(h, pool_w_in[i].astype(BF16), pool_w_grp[i].astype(BF16),
                            row(pool_scale[i]), pool_w_out[i].astype(BF16),
                            row(ln1_g[i]), row(ln1_b[i]), seq=seq)
        else:
            h3d = h.reshape(bsz, seq, d)
            if kv is None:
                kv = (_proj(h3d, shared_w_k.astype(BF16)), _proj(h3d, shared_w_v.astype(BF16)))
            qs = _proj(h3d, attn_w_q[i - n_a].astype(BF16))
            outs = []
            for q, k, v in zip(qs, kv[0], kv[1]):
                qb = min(q.shape[2], 4 * ATTN_BLOCK)
                outs.append(_attn_branch(q, k, v, qb=qb, heads_per_iter=16 * ATTN_BLOCK // qb))
            h = _attn_out([o for o, _ in outs], [l for _, l in outs], h3d,
                          attn_w_o[i - n_a].astype(BF16), row(ln1_g[i]), row(ln1_b[i]))
            h = h.reshape(bsz * seq, d)
        ffn = _ffn_params(ffn_w_up[i], ffn_conv_w[i], ffn_conv_b[i], ffn_w_down[i], tf)
        h = _ffn_layer(h, *ffn, row(ln2_g[i]), row(ln2_b[i]), seq=seq, tf=tf)
    return h.reshape(bsz, seq, d)
```

```python
import functools
import math

import jax
import jax.numpy as jnp
from jax import lax
from jax.experimental import pallas as pl
from jax.experimental.pallas import tpu as pltpu

POOL_WINDOWS = (2, 4, 8, 16)
HEAD_DIM = 128
DILATED_BRANCHES = ((128, 1), (512, 4), (2048, 16))
ATTN_BLOCK = 128
CONV_WIDTH = 3
DEPTH = 2
DEEPNORM_ALPHA = (2.0 * DEPTH) ** 0.25
LN_EPS = 1e-5
NEG_INF = -1e30

LANES = 128
SUBLANES = 8
MXU_DIM = 256
VMEM_LIMIT_BYTES = 56 * 1024 * 1024

DILATIONS = tuple(d for _, d in DILATED_BRANCHES)
POOL_HALO = max(POOL_WINDOWS)

F32 = jnp.float32
BF16 = jnp.bfloat16


def _layer_norm(y, g, b):
    mu = jnp.mean(y, axis=-1, keepdims=True)
    d = y - mu
    var = jnp.mean(d * d, axis=-1, keepdims=True)
    return d * lax.rsqrt(var + LN_EPS) * g + b


def _const_spec(shape):
    nd = len(shape)
    return pl.BlockSpec(shape, lambda *_: (0,) * nd, pipeline_mode=pl.Buffered(1))


def _pool_kernel(x_ref, win_ref, wgrp_ref, scale_ref, wout_ref, g_ref, b_ref, o_ref,
                 pbuf, mbuf, *, tm, blocks_per_seq):
    i = pl.program_id(0)
    blk = i % blocks_per_seq
    d_model = x_ref.shape[1]
    gdim = d_model // len(POOL_WINDOWS)

    @pl.when(blk == 0)
    def _():
        pbuf[0:POOL_HALO, :] = jnp.zeros((POOL_HALO, d_model), F32)

    @pl.when(blk != 0)
    def _():
        pbuf[0:POOL_HALO, :] = pbuf[tm:tm + POOL_HALO, :]

    x = x_ref[...]
    p = jnp.dot(x.astype(BF16), win_ref[...], preferred_element_type=F32)
    pbuf[POOL_HALO:POOL_HALO + tm, :] = p

    pos = blk * tm + lax.broadcasted_iota(jnp.int32, (tm, 1), 0)
    for g, w in enumerate(POOL_WINDOWS):
        cols = slice(g * gdim, (g + 1) * gdim)
        s = pbuf[:, cols]
        shift = 1
        while shift < w:
            s = s + pltpu.roll(s, shift, axis=0)
            shift *= 2
        cnt = jnp.minimum(pos + 1, w).astype(F32)
        pooled = s[POOL_HALO:, :] / cnt - p[:, cols]
        mixed = jnp.dot(pooled.astype(BF16), wgrp_ref[g], preferred_element_type=F32)
        mbuf[:, cols] = (mixed * scale_ref[:, cols]).astype(BF16)

    mix = jnp.dot(mbuf[...], wout_ref[...], preferred_element_type=F32)
    o_ref[...] = _layer_norm(DEEPNORM_ALPHA * x + mix, g_ref[...], b_ref[...])


def _pool_layer(x2d, w_in, w_grp, scale, w_out, ln_g, ln_b, *, seq, tm=512):
    t, d = x2d.shape
    ng, gdim, _ = w_grp.shape
    kern = functools.partial(_pool_kernel, tm=tm, blocks_per_seq=seq // tm)
    return pl.pallas_call(
        kern,
        grid=(t // tm,),
        in_specs=[
            pl.BlockSpec((tm, d), lambda i: (i, 0)),
            _const_spec((d, d)),
            _const_spec((ng, gdim, gdim)),
            _const_spec((1, d)),
            _const_spec((d, d)),
            _const_spec((1, d)),
            _const_spec((1, d)),
        ],
        out_specs=pl.BlockSpec((tm, d), lambda i: (i, 0)),
        out_shape=jax.ShapeDtypeStruct((t, d), F32),
        scratch_shapes=[
            pltpu.VMEM((POOL_HALO + tm, d), F32),
            pltpu.VMEM((tm, d), BF16),
        ],
        compiler_params=pltpu.CompilerParams(
            dimension_semantics=("arbitrary",), vmem_limit_bytes=VMEM_LIMIT_BYTES),
        name="pool_layer",
    )(x2d, w_in, w_grp, scale, w_out, ln_g, ln_b)


def _ffn_kernel(h_ref, wg_ref, wv_ref, cp_ref, wd_ref, g_ref, b_ref, o_ref,
                hb, acc, ug_buf, uv_buf, carry_g, carry_v, *act_bufs, tm, nf, blocks_per_seq):
    i = pl.program_id(0)
    f = pl.program_id(1)
    seq_start = (i % blocks_per_seq) == 0
    tf = wg_ref.shape[1]
    fp = nf * tf

    def conv(uc, buf, carry, cp_base, c):
        cs = slice(c * LANES, (c + 1) * LANES)
        cp = cp_ref[:, pl.ds(pl.multiple_of(cp_base + f * tf + c * LANES, LANES), LANES)]
        buf[c, 0:SUBLANES, :] = jnp.where(seq_start, 0.0, carry[f, :, cs])
        buf[c, SUBLANES:SUBLANES + tm, :] = uc
        carry[f, :, cs] = uc[tm - SUBLANES:tm, :]
        out = cp[CONV_WIDTH:CONV_WIDTH + 1, :] + uc * cp[CONV_WIDTH - 1:CONV_WIDTH, :]
        for lag in range(1, CONV_WIDTH):
            tap = CONV_WIDTH - 1 - lag
            out = out + buf[c, SUBLANES - lag:SUBLANES - lag + tm, :] * cp[tap:tap + 1, :]
        return out

    def up_stage(act_dst):
        ug = jnp.dot(hb[...], wg_ref[...], preferred_element_type=F32)
        uv = jnp.dot(hb[...], wv_ref[...], preferred_element_type=F32)
        for c in range(tf // LANES):
            cs = slice(c * LANES, (c + 1) * LANES)
            gate = conv(ug[:, cs], ug_buf, carry_g, 0, c)
            val = conv(uv[:, cs], uv_buf, carry_v, fp, c)
            act_dst[:, cs] = (gate / (1.0 + jnp.exp(-gate)) * val).astype(BF16)

    def down_stage(act_src):
        acc[...] += jnp.dot(act_src[...], wd_ref[...], preferred_element_type=F32)

    @pl.when(f == 0)
    def _():
        hb[...] = h_ref[...].astype(BF16)
        acc[...] = jnp.zeros(acc.shape, F32)
        up_stage(act_bufs[0])

    for parity in range(2):
        @pl.when((f > 0) & (f < nf) & (f % 2 == parity))
        def _():
            up_stage(act_bufs[parity])
            down_stage(act_bufs[1 - parity])

    @pl.when(f == nf)
    def _():
        down_stage(act_bufs[(nf - 1) % 2])
        y = DEEPNORM_ALPHA * h_ref[...] + acc[...]
        o_ref[...] = _layer_norm(y, g_ref[...], b_ref[...])


def _ffn_layer(h2d, w_gv, cp, w_down, ln_g, ln_b, *, layer, seq, tm=512, tf=512):
    t, d = h2d.shape
    fp = w_down.shape[1]
    nf = fp // tf
    kern = functools.partial(_ffn_kernel, tm=tm, nf=nf, blocks_per_seq=seq // tm)
    return pl.pallas_call(
        kern,
        grid=(t // tm, nf + 1),
        in_specs=[
            pl.BlockSpec((tm, d), lambda i, f: (i, 0)),
            pl.BlockSpec((None, d, tf), lambda i, f: (layer, 0, jnp.minimum(f, nf - 1))),
            pl.BlockSpec((None, d, tf), lambda i, f: (layer, 0, nf + jnp.minimum(f, nf - 1))),
            _const_spec((SUBLANES, 2 * fp)),
            pl.BlockSpec((None, tf, d), lambda i, f: (layer, jnp.maximum(f - 1, 0), 0)),
            _const_spec((1, d)),
            _const_spec((1, d)),
        ],
        out_specs=pl.BlockSpec((tm, d), lambda i, f: (i, 0)),
        out_shape=jax.ShapeDtypeStruct((t, d), F32),
        scratch_shapes=[
            pltpu.VMEM((tm, d), BF16),
            pltpu.VMEM((tm, d), F32),
            pltpu.VMEM((tf // LANES, SUBLANES + tm, LANES), F32),
            pltpu.VMEM((tf // LANES, SUBLANES + tm, LANES), F32),
            pltpu.VMEM((nf, SUBLANES, tf), F32),
            pltpu.VMEM((nf, SUBLANES, tf), F32),
            pltpu.VMEM((tm, tf), BF16),
            pltpu.VMEM((tm, tf), BF16),
        ],
        compiler_params=pltpu.CompilerParams(
            dimension_semantics=("arbitrary", "arbitrary"), vmem_limit_bytes=VMEM_LIMIT_BYTES),
        name="ffn_layer",
    )(h2d, w_gv, w_gv, cp, w_down, ln_g, ln_b)


def _lagged_block(nblocks, blocks_per_seq, lag):
    def split(g):
        j = jnp.clip(g - lag, 0, nblocks - 1)
        return j // blocks_per_seq, j % blocks_per_seq
    return split


def _nat_spec(tm, width, split):
    def idx(g):
        b, i = split(g)
        return (b, i, 0)
    return pl.BlockSpec((None, tm, width), idx)


def _dilated_spec(dil, tm, width, split):
    def idx(g):
        b, i = split(g)
        return (b, 0, i, 0)
    return pl.BlockSpec((None, dil, tm // dil, width), idx)


def _proj_kernel(x_ref, w_ref, *refs, nblocks):
    nat_ref, o4_ref, o16_ref = refs[:3]
    ybufs, zbuf = refs[3:5], refs[5]
    n_tiles, tm, _ = ybufs[0].shape
    g = pl.program_id(0)
    tiles_per_dot = MXU_DIM // LANES

    def matmul(ybuf, xb, n):
        cols = slice(n * MXU_DIM, (n + 1) * MXU_DIM)
        y = jnp.dot(xb, w_ref[:, cols], preferred_element_type=F32)
        nat_ref[0, :, cols] = y.astype(BF16)
        for t in range(tiles_per_dot):
            ybuf[n * tiles_per_dot + t] = y[:, t * LANES:(t + 1) * LANES]

    def scatter(ybuf, n):
        for c in range(n * tiles_per_dot, (n + 1) * tiles_per_dot):
            cs = slice(c * LANES, (c + 1) * LANES)
            for r in range(4):
                slab = ybuf[c, pl.ds(r, tm // 4, stride=4), :]
                o4_ref[r, :, cs] = slab.astype(BF16)
                zbuf[c, r] = slab
            for r in range(4):
                for q in range(4):
                    o16_ref[4 * q + r, :, cs] = (
                        zbuf[c, r, pl.ds(q, tm // 16, stride=4), :].astype(BF16))

    def step(y_dst, y_src):
        xb = None if y_dst is None else x_ref[...].astype(BF16)
        for n in range(n_tiles // tiles_per_dot):
            if y_dst is not None:
                matmul(y_dst, xb, n)
            if y_src is not None:
                scatter(y_src, n)

    @pl.when(g == 0)
    def _():
        step(ybufs[0], None)

    for parity in range(2):
        @pl.when((g > 0) & (g < nblocks) & (g % 2 == parity))
        def _():
            step(ybufs[parity], ybufs[1 - parity])

    @pl.when(g == nblocks)
    def _():
        step(None, ybufs[(nblocks - 1) % 2])


def _proj(h3d, w, *, tm=512):
    bsz, seq, d = h3d.shape
    blocks_per_seq = seq // tm
    nblocks = bsz * blocks_per_seq
    cur = _lagged_block(nblocks, blocks_per_seq, 0)
    prev = _lagged_block(nblocks, blocks_per_seq, 1)
    assert DILATIONS == (1, 4, 16), "the two-step row de-interleave is written for 1, 4, 16"
    ybuf = pltpu.VMEM((d // LANES, tm, LANES), F32)
    zbuf = pltpu.VMEM((d // LANES, 4, tm // 4, LANES), F32)
    return pl.pallas_call(
        functools.partial(_proj_kernel, nblocks=nblocks),
        grid=(nblocks + 1,),
        in_specs=[_nat_spec(tm, d, cur), _const_spec((d, d))],
        out_specs=[_dilated_spec(1, tm, d, cur), _dilated_spec(4, tm, d, prev),
                   _dilated_spec(16, tm, d, prev)],
        out_shape=[jax.ShapeDtypeStruct((bsz, dil, seq // dil, d), BF16) for dil in DILATIONS],
        scratch_shapes=[ybuf, ybuf, zbuf],
        compiler_params=pltpu.CompilerParams(
            dimension_semantics=("arbitrary",), vmem_limit_bytes=VMEM_LIMIT_BYTES),
        name="proj",
    )(h3d, w)


def _attn_kernel(q_ref, kp_ref, kc_ref, vp_ref, vc_ref, o_ref, lse_ref, bias,
                 *, qb, n_heads, heads_per_iter):
    i = pl.program_id(2)
    nblk = qb // ATTN_BLOCK
    blk = ATTN_BLOCK

    row = lax.broadcasted_iota(jnp.int32, (blk, 2 * blk), 0)
    col = lax.broadcasted_iota(jnp.int32, (blk, 2 * blk), 1)
    dist = blk + row - col
    band = (dist >= 0) & (dist <= blk)
    has_prev = jnp.broadcast_to(i > 0, band.shape)
    bias[1] = jnp.where(band, 0.0, NEG_INF)
    bias[0] = jnp.where(band & ((col >= blk) | has_prev), 0.0, NEG_INF)

    scale = 1.0 / math.sqrt(HEAD_DIM)
    log2e = math.log2(math.e)
    lane = lax.broadcasted_iota(jnp.int32, (blk, LANES), 1)
    ones = jnp.ones((2 * blk, HEAD_DIM), BF16)
    lse_ref[...] = jnp.zeros(lse_ref.shape, F32)

    def one_block(h, j):
        hc = pl.ds(pl.multiple_of(h * HEAD_DIM, HEAD_DIM), HEAD_DIM)
        rows = slice(j * blk, (j + 1) * blk)
        q = q_ref[rows, hc]
        if j == 0:
            kk = jnp.concatenate([kp_ref[:, hc], kc_ref[0:blk, hc]], axis=0)
            vv = jnp.concatenate([vp_ref[:, hc], vc_ref[0:blk, hc]], axis=0)
        else:
            kk = kc_ref[(j - 1) * blk:(j + 1) * blk, hc]
            vv = vc_ref[(j - 1) * blk:(j + 1) * blk, hc]
        s = lax.dot_general(q, kk, (((1,), (1,)), ((), ())), preferred_element_type=F32)
        s = s + bias[min(j, 1)]
        m = jnp.max(s, axis=-1, keepdims=True)
        p = jnp.exp2((s - m) * (scale * log2e))
        ov = jnp.dot(p.astype(BF16), jnp.concatenate([vv, ones], axis=1),
                     preferred_element_type=F32)
        den = ov[:, HEAD_DIM:]
        o_ref[rows, hc] = (ov[:, :HEAD_DIM] / den).astype(o_ref.dtype)
        lse = m * scale + jnp.log(den)
        lse_ref[rows, :] = jnp.where(lane == h, lse, lse_ref[rows, :])

    def head_group(g, carry):
        for hh in range(heads_per_iter):
            for j in range(nblk):
                one_block(g * heads_per_iter + hh, j)
        return carry

    lax.fori_loop(0, n_heads // heads_per_iter, head_group, 0)


def _attn_branch(q, k, v, *, qb, heads_per_iter):
    bsz, dil, length, d = q.shape
    n_heads = d // HEAD_DIM
    per = qb // ATTN_BLOCK
    cur = pl.BlockSpec((None, None, qb, d), lambda b, r, i: (b, r, i, 0))
    prev = pl.BlockSpec((None, None, ATTN_BLOCK, d),
                        lambda b, r, i: (b, r, jnp.maximum(i * per - 1, 0), 0))
    kern = functools.partial(_attn_kernel, qb=qb, n_heads=n_heads, heads_per_iter=heads_per_iter)
    return pl.pallas_call(
        kern,
        grid=(bsz, dil, length // qb),
        in_specs=[cur, prev, cur, prev, cur],
        out_specs=[cur, pl.BlockSpec((None, None, qb, LANES), lambda b, r, i: (b, r, i, 0))],
        out_shape=[
            jax.ShapeDtypeStruct((bsz, dil, length, d), BF16),
            jax.ShapeDtypeStruct((bsz, dil, length, LANES), F32),
        ],
        scratch_shapes=[pltpu.VMEM((2, ATTN_BLOCK, 2 * ATTN_BLOCK), F32)],
        compiler_params=pltpu.CompilerParams(
            dimension_semantics=("arbitrary", "arbitrary", "arbitrary"),
            vmem_limit_bytes=VMEM_LIMIT_BYTES),
        name=f"attn_d{dil}",
    )(q, k, k, v, v)


def _attn_out_kernel(o1_ref, o4_ref, o16_ref, l1_ref, l4_ref, l16_ref, h_ref, wo_ref, g_ref,
                     b_ref, out_ref, o_nat, z16, l_nat, zl16, ybuf, *xbufs, n_heads, nblocks):
    tm = h_ref.shape[0]
    g = pl.program_id(0)
    heads_per_dot = MXU_DIM // HEAD_DIM

    def to_natural(dst, src4):
        for r in range(4):
            dst[pl.ds(r, tm // 4, stride=4), :] = src4(r)

    def from_d16(tmp, src16):
        for r in range(4):
            for q in range(4):
                tmp[r, pl.ds(q, tm // 16, stride=4), :] = src16(4 * q + r)
        return lambda r: tmp[r]

    def merge_weights():
        to_natural(l_nat.at[0], lambda r: l4_ref[r])
        to_natural(l_nat.at[1], from_d16(zl16, lambda r: l16_ref[r]))
        lses = [l1_ref[0], l_nat[0], l_nat[1]]
        mx = functools.reduce(jnp.maximum, lses)
        es = [jnp.exp(l - mx) for l in lses]
        tot = functools.reduce(lambda a, b: a + b, es)
        return es[0] / tot, es[1] / tot

    def merge_head(xbuf, w1, w4, h):
        hc = slice(h * HEAD_DIM, (h + 1) * HEAD_DIM)
        to_natural(o_nat.at[0, h], lambda r: o4_ref[r, :, hc].astype(F32))
        to_natural(o_nat.at[1, h],
                   from_d16(z16.at[h], lambda r: o16_ref[r, :, hc].astype(F32)))
        o1 = o1_ref[0, :, hc].astype(F32)
        o4 = o_nat[0, h]
        o16 = o_nat[1, h]
        shape = (tm, HEAD_DIM)
        o = (o16 + jnp.broadcast_to(w1[:, h:h + 1], shape) * (o1 - o16)
             + jnp.broadcast_to(w4[:, h:h + 1], shape) * (o4 - o16))
        xbuf[:, hc] = o.astype(BF16)

    def step(x_dst, x_src):
        if x_dst is not None:
            w1, w4 = merge_weights()
        for n in range(n_heads // heads_per_dot):
            if x_src is not None:
                cols = slice(n * MXU_DIM, (n + 1) * MXU_DIM)
                ybuf[:, cols] = jnp.dot(x_src[...], wo_ref[:, cols], preferred_element_type=F32)
            if x_dst is not None:
                for h in range(n * heads_per_dot, (n + 1) * heads_per_dot):
                    merge_head(x_dst, w1, w4, h)
        if x_src is not None:
            y = DEEPNORM_ALPHA * h_ref[...] + ybuf[...]
            out_ref[...] = _layer_norm(y, g_ref[...], b_ref[...])

    @pl.when(g == 0)
    def _():
        step(xbufs[0], None)

    for parity in range(2):
        @pl.when((g > 0) & (g < nblocks) & (g % 2 == parity))
        def _():
            step(xbufs[parity], xbufs[1 - parity])

    @pl.when(g == nblocks)
    def _():
        step(None, xbufs[(nblocks - 1) % 2])


def _attn_out(os, ls, h3d, w_o, ln_g, ln_b, *, tm=256):
    bsz, seq, d = h3d.shape
    assert DILATIONS == (1, 4, 16), "the two-step row interleave is written for 1, 4, 16"
    n_heads = d // HEAD_DIM
    blocks_per_seq = seq // tm
    nblocks = bsz * blocks_per_seq
    cur = _lagged_block(nblocks, blocks_per_seq, 0)
    prev = _lagged_block(nblocks, blocks_per_seq, 1)
    xbuf = pltpu.VMEM((tm, d), BF16)
    kern = functools.partial(_attn_out_kernel, n_heads=n_heads, nblocks=nblocks)
    return pl.pallas_call(
        kern,
        grid=(nblocks + 1,),
        in_specs=([_dilated_spec(dil, tm, d, cur) for dil in DILATIONS]
                  + [_dilated_spec(dil, tm, LANES, cur) for dil in DILATIONS]
                  + [_nat_spec(tm, d, prev), _const_spec((d, d)), _const_spec((1, d)),
                     _const_spec((1, d))]),
        out_specs=_nat_spec(tm, d, prev),
        out_shape=jax.ShapeDtypeStruct((bsz, seq, d), F32),
        scratch_shapes=[
            pltpu.VMEM((2, n_heads, tm, HEAD_DIM), F32),
            pltpu.VMEM((n_heads, 4, tm // 4, HEAD_DIM), F32),
            pltpu.VMEM((2, tm, LANES), F32),
            pltpu.VMEM((4, tm // 4, LANES), F32),
            pltpu.VMEM((tm, d), F32),
            xbuf, xbuf,
        ],
        compiler_params=pltpu.CompilerParams(
            dimension_semantics=("arbitrary",), vmem_limit_bytes=VMEM_LIMIT_BYTES),
        name="attn_out",
    )(*os, *ls, h3d, w_o, ln_g, ln_b)


def _cast_pad_kernel(x_ref, o_ref, *, axis):
    n = x_ref.shape[axis]
    head = tuple(slice(0, n) if a == axis else slice(None) for a in range(2))
    tail = tuple(slice(n, None) if a == axis else slice(None) for a in range(2))
    o_ref[head] = x_ref[...].astype(BF16)
    if o_ref.shape[axis] > n:
        o_ref[tail] = jnp.zeros(o_ref[tail].shape, BF16)


def _ffn_weights(w_up, w_down, tf, *, rows=256, cols=256):
    n_layers, d, two_f = w_up.shape
    f = two_f // 2
    fp = f + (-f % tf)
    params = pltpu.CompilerParams(dimension_semantics=("arbitrary",) * 3,
                                  vmem_limit_bytes=VMEM_LIMIT_BYTES)
    w_gv = pl.pallas_call(
        functools.partial(_cast_pad_kernel, axis=1),
        grid=(n_layers, d // rows, 2),
        in_specs=[pl.BlockSpec((None, rows, f), lambda l, r, j: (l, r, j))],
        out_specs=pl.BlockSpec((None, rows, fp), lambda l, r, j: (l, r, j)),
        out_shape=jax.ShapeDtypeStruct((n_layers, d, 2 * fp), BF16),
        compiler_params=params,
        name="ffn_w_up_bf16",
    )(w_up)
    w_dn = pl.pallas_call(
        functools.partial(_cast_pad_kernel, axis=0),
        grid=(n_layers, d // cols, 1),
        in_specs=[pl.BlockSpec((None, f, cols), lambda l, c, _: (l, 0, c))],
        out_specs=pl.BlockSpec((None, fp, cols), lambda l, c, _: (l, 0, c)),
        out_shape=jax.ShapeDtypeStruct((n_layers, fp, d), BF16),
        compiler_params=params,
        name="ffn_w_down_bf16",
    )(w_down)
    return w_gv, w_dn, fp


def _conv_params(conv_w, conv_b, fp):
    two_f = conv_w.shape[1]
    f = two_f // 2
    fill = jnp.zeros((SUBLANES - CONV_WIDTH - 1, two_f), F32)
    cp = jnp.concatenate([conv_w, conv_b[None, :], fill], axis=0)
    z = jnp.zeros((SUBLANES, fp - f), F32)
    return jnp.concatenate([cp[:, :f], z, cp[:, f:], z], axis=1)


def kernel(x, pool_w_in, pool_w_grp, pool_scale, pool_w_out, attn_w_q, attn_w_o, shared_w_k,
           shared_w_v, ffn_w_up, ffn_conv_w, ffn_conv_b, ffn_w_down, ln1_g, ln1_b, ln2_g, ln2_b):
    bsz, seq, d = x.shape
    n_a = pool_w_in.shape[0]
    n_layers = ffn_w_up.shape[0]
    tf = 2 * MXU_DIM
    row = lambda a: a.reshape(1, d)

    w_gv, w_dn, fp = _ffn_weights(ffn_w_up, ffn_w_down, tf)
    h = x.reshape(bsz * seq, d)
    kv = None
    for i in range(n_layers):
        if i < n_a:
            h = _pool_layer(h, pool_w_in[i].astype(BF16), pool_w_grp[i].astype(BF16),
                            row(pool_scale[i]), pool_w_out[i].astype(BF16),
                            row(ln1_g[i]), row(ln1_b[i]), seq=seq)
        else:
            h3d = h.reshape(bsz, seq, d)
            if kv is None:
                kv = (_proj(h3d, shared_w_k.astype(BF16)), _proj(h3d, shared_w_v.astype(BF16)))
            qs = _proj(h3d, attn_w_q[i - n_a].astype(BF16))
            outs = []
            for q, k, v in zip(qs, kv[0], kv[1]):
                qb = min(q.shape[2], 8 * ATTN_BLOCK)
                outs.append(_attn_branch(q, k, v, qb=qb, heads_per_iter=16 * ATTN_BLOCK // qb))
            h = _attn_out([o for o, _ in outs], [l for _, l in outs], h3d,
                          attn_w_o[i - n_a].astype(BF16), row(ln1_g[i]), row(ln1_b[i]))
            h = h.reshape(bsz * seq, d)
        cp = _conv_params(ffn_conv_w[i], ffn_conv_b[i], fp)
        h = _ffn_layer(h, w_gv, cp, w_dn, row(ln2_g[i]), row(ln2_b[i]), layer=i, seq=seq, tf=tf)
    return h.reshape(bsz, seq, d)
```

```python
import functools
import math

import jax
import jax.numpy as jnp
from jax import lax
from jax.experimental import pallas as pl
from jax.experimental.pallas import tpu as pltpu

POOL_WINDOWS = (2, 4, 8, 16)
HEAD_DIM = 128
DILATED_BRANCHES = ((128, 1), (512, 4), (2048, 16))
ATTN_BLOCK = 128
CONV_WIDTH = 3
DEPTH = 2
DEEPNORM_ALPHA = (2.0 * DEPTH) ** 0.25
LN_EPS = 1e-5
NEG_INF = -1e30

LANES = 128
SUBLANES = 8
MXU_DIM = 256
VMEM_LIMIT_BYTES = 56 * 1024 * 1024

DILATIONS = tuple(d for _, d in DILATED_BRANCHES)
POOL_HALO = max(POOL_WINDOWS)

F32 = jnp.float32
BF16 = jnp.bfloat16


def _layer_norm(y, g, b):
    mu = jnp.mean(y, axis=-1, keepdims=True)
    d = y - mu
    var = jnp.mean(d * d, axis=-1, keepdims=True)
    return d * lax.rsqrt(var + LN_EPS) * g + b


def _const_spec(shape):
    nd = len(shape)
    return pl.BlockSpec(shape, lambda *_: (0,) * nd, pipeline_mode=pl.Buffered(1))


def _pool_kernel(x_ref, win_ref, wgrp_ref, scale_ref, wout_ref, g_ref, b_ref, o_ref,
                 pbuf, mbuf, *, tm, blocks_per_seq):
    i = pl.program_id(0)
    blk = i % blocks_per_seq
    d_model = x_ref.shape[1]
    gdim = d_model // len(POOL_WINDOWS)

    @pl.when(blk == 0)
    def _():
        pbuf[0:POOL_HALO, :] = jnp.zeros((POOL_HALO, d_model), F32)

    @pl.when(blk != 0)
    def _():
        pbuf[0:POOL_HALO, :] = pbuf[tm:tm + POOL_HALO, :]

    x = x_ref[...]
    p = jnp.dot(x.astype(BF16), win_ref[...], preferred_element_type=F32)
    pbuf[POOL_HALO:POOL_HALO + tm, :] = p

    pos = blk * tm + lax.broadcasted_iota(jnp.int32, (tm, 1), 0)
    for g, w in enumerate(POOL_WINDOWS):
        cols = slice(g * gdim, (g + 1) * gdim)
        s = pbuf[:, cols]
        shift = 1
        while shift < w:
            s = s + pltpu.roll(s, shift, axis=0)
            shift *= 2
        cnt = jnp.minimum(pos + 1, w).astype(F32)
        pooled = s[POOL_HALO:, :] / cnt - p[:, cols]
        mixed = jnp.dot(pooled.astype(BF16), wgrp_ref[g], preferred_element_type=F32)
        mbuf[:, cols] = (mixed * scale_ref[:, cols]).astype(BF16)

    mix = jnp.dot(mbuf[...], wout_ref[...], preferred_element_type=F32)
    o_ref[...] = _layer_norm(DEEPNORM_ALPHA * x + mix, g_ref[...], b_ref[...])


def _pool_layer(x2d, w_in, w_grp, scale, w_out, ln_g, ln_b, *, seq, tm=512):
    t, d = x2d.shape
    ng, gdim, _ = w_grp.shape
    kern = functools.partial(_pool_kernel, tm=tm, blocks_per_seq=seq // tm)
    return pl.pallas_call(
        kern,
        grid=(t // tm,),
        in_specs=[
            pl.BlockSpec((tm, d), lambda i: (i, 0)),
            _const_spec((d, d)),
            _const_spec((ng, gdim, gdim)),
            _const_spec((1, d)),
            _const_spec((d, d)),
            _const_spec((1, d)),
            _const_spec((1, d)),
        ],
        out_specs=pl.BlockSpec((tm, d), lambda i: (i, 0)),
        out_shape=jax.ShapeDtypeStruct((t, d), F32),
        scratch_shapes=[
            pltpu.VMEM((POOL_HALO + tm, d), F32),
            pltpu.VMEM((tm, d), BF16),
        ],
        compiler_params=pltpu.CompilerParams(
            dimension_semantics=("arbitrary",), vmem_limit_bytes=VMEM_LIMIT_BYTES),
        name="pool_layer",
    )(x2d, w_in, w_grp, scale, w_out, ln_g, ln_b)


def _ffn_kernel(h_ref, wg_ref, wv_ref, cp_ref, wd_ref, g_ref, b_ref, *refs,
                tm, nf, blocks_per_seq, n_out):
    out_refs = refs[:n_out]
    hb, acc, ug_buf, uv_buf, carry_g, carry_v, *act_bufs = refs[n_out:]
    i = pl.program_id(0)
    f = pl.program_id(1)
    seq_start = (i % blocks_per_seq) == 0
    tf = wg_ref.shape[1]
    fp = nf * tf

    def conv(uc, buf, carry, cp_base, c):
        cs = slice(c * LANES, (c + 1) * LANES)
        cp = cp_ref[:, pl.ds(pl.multiple_of(cp_base + f * tf + c * LANES, LANES), LANES)]
        buf[c, 0:SUBLANES, :] = jnp.where(seq_start, 0.0, carry[f, :, cs])
        buf[c, SUBLANES:SUBLANES + tm, :] = uc
        carry[f, :, cs] = uc[tm - SUBLANES:tm, :]
        out = cp[CONV_WIDTH:CONV_WIDTH + 1, :] + uc * cp[CONV_WIDTH - 1:CONV_WIDTH, :]
        for lag in range(1, CONV_WIDTH):
            tap = CONV_WIDTH - 1 - lag
            out = out + buf[c, SUBLANES - lag:SUBLANES - lag + tm, :] * cp[tap:tap + 1, :]
        return out

    def up_stage(act_dst):
        ug = jnp.dot(hb[...], wg_ref[...], preferred_element_type=F32)
        uv = jnp.dot(hb[...], wv_ref[...], preferred_element_type=F32)
        for c in range(tf // LANES):
            cs = slice(c * LANES, (c + 1) * LANES)
            gate = conv(ug[:, cs], ug_buf, carry_g, 0, c)
            val = conv(uv[:, cs], uv_buf, carry_v, fp, c)
            sig_den = 1.0 + jnp.exp2(gate * (-math.log2(math.e)))
            act_dst[:, cs] = (gate / sig_den * val).astype(BF16)

    def down_stage(act_src):
        acc[...] += jnp.dot(act_src[...], wd_ref[...], preferred_element_type=F32)

    @pl.when(f == 0)
    def _():
        hb[...] = h_ref[...].astype(BF16)
        acc[...] = jnp.zeros(acc.shape, F32)
        up_stage(act_bufs[0])

    for parity in range(2):
        @pl.when((f > 0) & (f < nf) & (f % 2 == parity))
        def _():
            up_stage(act_bufs[parity])
            down_stage(act_bufs[1 - parity])

    @pl.when(f == nf)
    def _():
        down_stage(act_bufs[(nf - 1) % 2])
        y = DEEPNORM_ALPHA * h_ref[...] + acc[...]
        out = _layer_norm(y, g_ref[...], b_ref[...])
        for o_ref in out_refs:
            o_ref[...] = out.astype(o_ref.dtype)


def _ffn_layer(h2d, w_gv, cp, w_down, ln_g, ln_b, *, layer, seq, bf16_copy, tm=512, tf=512):
    t, d = h2d.shape
    fp = w_down.shape[1]
    nf = fp // tf
    out_dtypes = (F32, BF16) if bf16_copy else (F32,)
    kern = functools.partial(_ffn_kernel, tm=tm, nf=nf, blocks_per_seq=seq // tm,
                             n_out=len(out_dtypes))
    return pl.pallas_call(
        kern,
        grid=(t // tm, nf + 1),
        in_specs=[
            pl.BlockSpec((tm, d), lambda i, f: (i, 0)),
            pl.BlockSpec((None, d, tf), lambda i, f: (layer, 0, jnp.minimum(f, nf - 1))),
            pl.BlockSpec((None, d, tf), lambda i, f: (layer, 0, nf + jnp.minimum(f, nf - 1))),
            _const_spec((SUBLANES, 2 * fp)),
            pl.BlockSpec((None, tf, d), lambda i, f: (layer, jnp.maximum(f - 1, 0), 0)),
            _const_spec((1, d)),
            _const_spec((1, d)),
        ],
        out_specs=[pl.BlockSpec((tm, d), lambda i, f: (i, 0)) for _ in out_dtypes],
        out_shape=[jax.ShapeDtypeStruct((t, d), dt) for dt in out_dtypes],
        scratch_shapes=[
            pltpu.VMEM((tm, d), BF16),
            pltpu.VMEM((tm, d), F32),
            pltpu.VMEM((tf // LANES, SUBLANES + tm, LANES), F32),
            pltpu.VMEM((tf // LANES, SUBLANES + tm, LANES), F32),
            pltpu.VMEM((nf, SUBLANES, tf), F32),
            pltpu.VMEM((nf, SUBLANES, tf), F32),
            pltpu.VMEM((tm, tf), BF16),
            pltpu.VMEM((tm, tf), BF16),
        ],
        compiler_params=pltpu.CompilerParams(
            dimension_semantics=("arbitrary", "arbitrary"), vmem_limit_bytes=VMEM_LIMIT_BYTES),
        name="ffn_layer",
    )(h2d, w_gv, w_gv, cp, w_down, ln_g, ln_b)


def _lagged_block(nblocks, blocks_per_seq, lag):
    def split(g):
        j = jnp.clip(g - lag, 0, nblocks - 1)
        return j // blocks_per_seq, j % blocks_per_seq
    return split


def _nat_spec(tm, width, split):
    def idx(g):
        b, i = split(g)
        return (b, i, 0)
    return pl.BlockSpec((None, tm, width), idx)


def _dilated_spec(dil, tm, width, split):
    def idx(g):
        b, i = split(g)
        return (b, 0, i, 0)
    return pl.BlockSpec((None, dil, tm // dil, width), idx)


def _proj_kernel(x_ref, w_ref, *refs, nblocks):
    nat_ref, o4_ref, o16_ref = refs[:3]
    ybufs, zbuf = refs[3:5], refs[5]
    n_tiles, tm, _ = ybufs[0].shape
    g = pl.program_id(0)
    tiles_per_dot = MXU_DIM // LANES

    def matmul(ybuf, xb, n):
        cols = slice(n * MXU_DIM, (n + 1) * MXU_DIM)
        y = jnp.dot(xb, w_ref[:, cols], preferred_element_type=F32)
        nat_ref[0, :, cols] = y.astype(BF16)
        for t in range(tiles_per_dot):
            ybuf[n * tiles_per_dot + t] = y[:, t * LANES:(t + 1) * LANES]

    def scatter(ybuf, n):
        for c in range(n * tiles_per_dot, (n + 1) * tiles_per_dot):
            cs = slice(c * LANES, (c + 1) * LANES)
            for r in range(4):
                slab = ybuf[c, pl.ds(r, tm // 4, stride=4), :]
                o4_ref[r, :, cs] = slab.astype(BF16)
                zbuf[c, r] = slab
            for r in range(4):
                for q in range(4):
                    o16_ref[4 * q + r, :, cs] = (
                        zbuf[c, r, pl.ds(q, tm // 16, stride=4), :].astype(BF16))

    def step(y_dst, y_src):
        xb = None if y_dst is None else x_ref[...].astype(BF16)
        for n in range(n_tiles // tiles_per_dot):
            if y_dst is not None:
                matmul(y_dst, xb, n)
            if y_src is not None:
                scatter(y_src, n)

    @pl.when(g == 0)
    def _():
        step(ybufs[0], None)

    for parity in range(2):
        @pl.when((g > 0) & (g < nblocks) & (g % 2 == parity))
        def _():
            step(ybufs[parity], ybufs[1 - parity])

    @pl.when(g == nblocks)
    def _():
        step(None, ybufs[(nblocks - 1) % 2])


def _proj(h3d, w, *, tm=512):
    bsz, seq, d = h3d.shape
    blocks_per_seq = seq // tm
    nblocks = bsz * blocks_per_seq
    cur = _lagged_block(nblocks, blocks_per_seq, 0)
    prev = _lagged_block(nblocks, blocks_per_seq, 1)
    assert DILATIONS == (1, 4, 16), "the two-step row de-interleave is written for 1, 4, 16"
    ybuf = pltpu.VMEM((d // LANES, tm, LANES), F32)
    zbuf = pltpu.VMEM((d // LANES, 4, tm // 4, LANES), F32)
    return pl.pallas_call(
        functools.partial(_proj_kernel, nblocks=nblocks),
        grid=(nblocks + 1,),
        in_specs=[_nat_spec(tm, d, cur), _const_spec((d, d))],
        out_specs=[_dilated_spec(1, tm, d, cur), _dilated_spec(4, tm, d, prev),
                   _dilated_spec(16, tm, d, prev)],
        out_shape=[jax.ShapeDtypeStruct((bsz, dil, seq // dil, d), BF16) for dil in DILATIONS],
        scratch_shapes=[ybuf, ybuf, zbuf],
        compiler_params=pltpu.CompilerParams(
            dimension_semantics=("arbitrary",), vmem_limit_bytes=VMEM_LIMIT_BYTES),
        name="proj",
    )(h3d, w)


def _attn_kernel(q_ref, kp_ref, kc_ref, vp_ref, vc_ref, o_ref, lse_ref, bias,
                 *, qb, n_heads, heads_per_iter):
    i = pl.program_id(2)
    nblk = qb // ATTN_BLOCK
    blk = ATTN_BLOCK

    row = lax.broadcasted_iota(jnp.int32, (blk, 2 * blk), 0)
    col = lax.broadcasted_iota(jnp.int32, (blk, 2 * blk), 1)
    dist = blk + row - col
    band = (dist >= 0) & (dist <= blk)
    has_prev = jnp.broadcast_to(i > 0, band.shape)
    bias[1] = jnp.where(band, 0.0, NEG_INF)
    bias[0] = jnp.where(band & ((col >= blk) | has_prev), 0.0, NEG_INF)

    scale = 1.0 / math.sqrt(HEAD_DIM)
    log2e = math.log2(math.e)
    lane = lax.broadcasted_iota(jnp.int32, (blk, LANES), 1)
    ones = jnp.ones((2 * blk, HEAD_DIM), BF16)
    lse_ref[...] = jnp.zeros(lse_ref.shape, F32)

    def one_block(h, j):
        hc = pl.ds(pl.multiple_of(h * HEAD_DIM, HEAD_DIM), HEAD_DIM)
        rows = slice(j * blk, (j + 1) * blk)
        q = q_ref[rows, hc]
        if j == 0:
            kk = jnp.concatenate([kp_ref[:, hc], kc_ref[0:blk, hc]], axis=0)
            vv = jnp.concatenate([vp_ref[:, hc], vc_ref[0:blk, hc]], axis=0)
        else:
            kk = kc_ref[(j - 1) * blk:(j + 1) * blk, hc]
            vv = vc_ref[(j - 1) * blk:(j + 1) * blk, hc]
        s = lax.dot_general(q, kk, (((1,), (1,)), ((), ())), preferred_element_type=F32)
        s = s + bias[min(j, 1)]
        m = jnp.max(s, axis=-1, keepdims=True)
        p = jnp.exp2((s - m) * (scale * log2e))
        ov = jnp.dot(p.astype(BF16), jnp.concatenate([vv, ones], axis=1),
                     preferred_element_type=F32)
        den = ov[:, HEAD_DIM:]
        o_ref[rows, hc] = (ov[:, :HEAD_DIM] / den).astype(o_ref.dtype)
        lse = m * scale + jnp.log(den)
        lse_ref[rows, :] = jnp.where(lane == h, lse, lse_ref[rows, :])

    def head_group(g, carry):
        for hh in range(heads_per_iter):
            for j in range(nblk):
                one_block(g * heads_per_iter + hh, j)
        return carry

    lax.fori_loop(0, n_heads // heads_per_iter, head_group, 0)


def _attn_branch(q, k, v, *, qb, heads_per_iter):
    bsz, dil, length, d = q.shape
    n_heads = d // HEAD_DIM
    per = qb // ATTN_BLOCK
    cur = pl.BlockSpec((None, None, qb, d), lambda b, r, i: (b, r, i, 0))
    prev = pl.BlockSpec((None, None, ATTN_BLOCK, d),
                        lambda b, r, i: (b, r, jnp.maximum(i * per - 1, 0), 0))
    kern = functools.partial(_attn_kernel, qb=qb, n_heads=n_heads, heads_per_iter=heads_per_iter)
    return pl.pallas_call(
        kern,
        grid=(bsz, dil, length // qb),
        in_specs=[cur, prev, cur, prev, cur],
        out_specs=[cur, pl.BlockSpec((None, None, qb, LANES), lambda b, r, i: (b, r, i, 0))],
        out_shape=[
            jax.ShapeDtypeStruct((bsz, dil, length, d), BF16),
            jax.ShapeDtypeStruct((bsz, dil, length, LANES), F32),
        ],
        scratch_shapes=[pltpu.VMEM((2, ATTN_BLOCK, 2 * ATTN_BLOCK), F32)],
        compiler_params=pltpu.CompilerParams(
            dimension_semantics=("arbitrary", "arbitrary", "arbitrary"),
            vmem_limit_bytes=VMEM_LIMIT_BYTES),
        name=f"attn_d{dil}",
    )(q, k, k, v, v)


def _attn_out_kernel(o1_ref, o4_ref, o16_ref, l1_ref, l4_ref, l16_ref, h_ref, wo_ref, g_ref,
                     b_ref, out_ref, o_nat, z16, l_nat, zl16, ybuf, *xbufs, n_heads, nblocks):
    tm = h_ref.shape[0]
    g = pl.program_id(0)
    heads_per_dot = MXU_DIM // HEAD_DIM

    def to_natural(dst, src4):
        for r in range(4):
            dst[pl.ds(r, tm // 4, stride=4), :] = src4(r)

    def from_d16(tmp, src16):
        for r in range(4):
            for q in range(4):
                tmp[r, pl.ds(q, tm // 16, stride=4), :] = src16(4 * q + r)
        return lambda r: tmp[r]

    def merge_weights():
        to_natural(l_nat.at[0], lambda r: l4_ref[r])
        to_natural(l_nat.at[1], from_d16(zl16, lambda r: l16_ref[r]))
        lses = [l1_ref[0], l_nat[0], l_nat[1]]
        mx = functools.reduce(jnp.maximum, lses)
        es = [jnp.exp(l - mx) for l in lses]
        tot = functools.reduce(lambda a, b: a + b, es)
        return es[0] / tot, es[1] / tot

    def merge_head(xbuf, w1, w4, h):
        hc = slice(h * HEAD_DIM, (h + 1) * HEAD_DIM)
        to_natural(o_nat.at[0, h], lambda r: o4_ref[r, :, hc].astype(F32))
        to_natural(o_nat.at[1, h],
                   from_d16(z16.at[h], lambda r: o16_ref[r, :, hc].astype(F32)))
        o1 = o1_ref[0, :, hc].astype(F32)
        o4 = o_nat[0, h]
        o16 = o_nat[1, h]
        shape = (tm, HEAD_DIM)
        o = (o16 + jnp.broadcast_to(w1[:, h:h + 1], shape) * (o1 - o16)
             + jnp.broadcast_to(w4[:, h:h + 1], shape) * (o4 - o16))
        xbuf[:, hc] = o.astype(BF16)

    def step(x_dst, x_src):
        if x_dst is not None:
            w1, w4 = merge_weights()
        for n in range(n_heads // heads_per_dot):
            if x_src is not None:
                cols = slice(n * MXU_DIM, (n + 1) * MXU_DIM)
                ybuf[:, cols] = jnp.dot(x_src[...], wo_ref[:, cols], preferred_element_type=F32)
            if x_dst is not None:
                for h in range(n * heads_per_dot, (n + 1) * heads_per_dot):
                    merge_head(x_dst, w1, w4, h)
        if x_src is not None:
            y = DEEPNORM_ALPHA * h_ref[...] + ybuf[...]
            out_ref[...] = _layer_norm(y, g_ref[...], b_ref[...])

    @pl.when(g == 0)
    def _():
        step(xbufs[0], None)

    for parity in range(2):
        @pl.when((g > 0) & (g < nblocks) & (g % 2 == parity))
        def _():
            step(xbufs[parity], xbufs[1 - parity])

    @pl.when(g == nblocks)
    def _():
        step(None, xbufs[(nblocks - 1) % 2])


def _attn_out(os, ls, h3d, w_o, ln_g, ln_b, *, tm=256):
    bsz, seq, d = h3d.shape
    assert DILATIONS == (1, 4, 16), "the two-step row interleave is written for 1, 4, 16"
    n_heads = d // HEAD_DIM
    blocks_per_seq = seq // tm
    nblocks = bsz * blocks_per_seq
    cur = _lagged_block(nblocks, blocks_per_seq, 0)
    prev = _lagged_block(nblocks, blocks_per_seq, 1)
    xbuf = pltpu.VMEM((tm, d), BF16)
    kern = functools.partial(_attn_out_kernel, n_heads=n_heads, nblocks=nblocks)
    return pl.pallas_call(
        kern,
        grid=(nblocks + 1,),
        in_specs=([_dilated_spec(dil, tm, d, cur) for dil in DILATIONS]
                  + [_dilated_spec(dil, tm, LANES, cur) for dil in DILATIONS]
                  + [_nat_spec(tm, d, prev), _const_spec((d, d)), _const_spec((1, d)),
                     _const_spec((1, d))]),
        out_specs=_nat_spec(tm, d, prev),
        out_shape=jax.ShapeDtypeStruct((bsz, seq, d), F32),
        scratch_shapes=[
            pltpu.VMEM((2, n_heads, tm, HEAD_DIM), F32),
            pltpu.VMEM((n_heads, 4, tm // 4, HEAD_DIM), F32),
            pltpu.VMEM((2, tm, LANES), F32),
            pltpu.VMEM((4, tm // 4, LANES), F32),
            pltpu.VMEM((tm, d), F32),
            xbuf, xbuf,
        ],
        compiler_params=pltpu.CompilerParams(
            dimension_semantics=("arbitrary",), vmem_limit_bytes=VMEM_LIMIT_BYTES),
        name="attn_out",
    )(*os, *ls, h3d, w_o, ln_g, ln_b)


def _cast_pad_kernel(x_ref, o_ref, *, axis):
    n = x_ref.shape[axis]
    head = tuple(slice(0, n) if a == axis else slice(None) for a in range(2))
    tail = tuple(slice(n, None) if a == axis else slice(None) for a in range(2))
    o_ref[head] = x_ref[...].astype(BF16)
    if o_ref.shape[axis] > n:
        o_ref[tail] = jnp.zeros(o_ref[tail].shape, BF16)


def _ffn_weights(w_up, w_down, tf, *, rows=256, cols=256):
    n_layers, d, two_f = w_up.shape
    f = two_f // 2
    fp = f + (-f % tf)
    params = pltpu.CompilerParams(dimension_semantics=("arbitrary",) * 3,
                                  vmem_limit_bytes=VMEM_LIMIT_BYTES)
    w_gv = pl.pallas_call(
        functools.partial(_cast_pad_kernel, axis=1),
        grid=(n_layers, d // rows, 2),
        in_specs=[pl.BlockSpec((None, rows, f), lambda l, r, j: (l, r, j))],
        out_specs=pl.BlockSpec((None, rows, fp), lambda l, r, j: (l, r, j)),
        out_shape=jax.ShapeDtypeStruct((n_layers, d, 2 * fp), BF16),
        compiler_params=params,
        name="ffn_w_up_bf16",
    )(w_up)
    w_dn = pl.pallas_call(
        functools.partial(_cast_pad_kernel, axis=0),
        grid=(n_layers, d // cols, 1),
        in_specs=[pl.BlockSpec((None, f, cols), lambda l, c, _: (l, 0, c))],
        out_specs=pl.BlockSpec((None, fp, cols), lambda l, c, _: (l, 0, c)),
        out_shape=jax.ShapeDtypeStruct((n_layers, fp, d), BF16),
        compiler_params=params,
        name="ffn_w_down_bf16",
    )(w_down)
    return w_gv, w_dn, fp


def _conv_params(conv_w, conv_b, fp):
    two_f = conv_w.shape[1]
    f = two_f // 2
    fill = jnp.zeros((SUBLANES - CONV_WIDTH - 1, two_f), F32)
    cp = jnp.concatenate([conv_w, conv_b[None, :], fill], axis=0)
    z = jnp.zeros((SUBLANES, fp - f), F32)
    return jnp.concatenate([cp[:, :f], z, cp[:, f:], z], axis=1)


def kernel(x, pool_w_in, pool_w_grp, pool_scale, pool_w_out, attn_w_q, attn_w_o, shared_w_k,
           shared_w_v, ffn_w_up, ffn_conv_w, ffn_conv_b, ffn_w_down, ln1_g, ln1_b, ln2_g, ln2_b):
    bsz, seq, d = x.shape
    n_a = pool_w_in.shape[0]
    n_layers = ffn_w_up.shape[0]
    tf = 2 * MXU_DIM
    row = lambda a: a.reshape(1, d)

    w_gv, w_dn, fp = _ffn_weights(ffn_w_up, ffn_w_down, tf)
    h = x.reshape(bsz * seq, d)
    h_bf = None
    kv = None
    for i in range(n_layers):
        if i < n_a:
            h = _pool_layer(h, pool_w_in[i].astype(BF16), pool_w_grp[i].astype(BF16),
                            row(pool_scale[i]), pool_w_out[i].astype(BF16),
                            row(ln1_g[i]), row(ln1_b[i]), seq=seq)
        else:
            h3d = h.reshape(bsz, seq, d)
            x3d = h3d if h_bf is None else h_bf.reshape(bsz, seq, d)
            if kv is None:
                kv = (_proj(x3d, shared_w_k.astype(BF16)), _proj(x3d, shared_w_v.astype(BF16)))
            qs = _proj(x3d, attn_w_q[i - n_a].astype(BF16))
            outs = []
            for q, k, v in zip(qs, kv[0], kv[1]):
                qb = min(q.shape[2], 8 * ATTN_BLOCK)
                outs.append(_attn_branch(q, k, v, qb=qb, heads_per_iter=32 * ATTN_BLOCK // qb))
            h = _attn_out([o for o, _ in outs], [l for _, l in outs], h3d,
                          attn_w_o[i - n_a].astype(BF16), row(ln1_g[i]), row(ln1_b[i]))
            h = h.reshape(bsz * seq, d)
        cp = _conv_params(ffn_conv_w[i], ffn_conv_b[i], fp)
        feeds_attention = n_a <= i + 1 < n_layers
        outs = _ffn_layer(h, w_gv, cp, w_dn, row(ln2_g[i]), row(ln2_b[i]), layer=i, seq=seq,
                          bf16_copy=feeds_attention, tf=tf)
        h, h_bf = (outs[0], outs[1]) if feeds_attention else (outs[0], None)
    return h.reshape(bsz, seq, d)
```

```python
import functools
import math

import jax
import jax.numpy as jnp
from jax import lax
from jax.experimental import pallas as pl
from jax.experimental.pallas import tpu as pltpu

POOL_WINDOWS = (2, 4, 8, 16)
HEAD_DIM = 128
DILATED_BRANCHES = ((128, 1), (512, 4), (2048, 16))
ATTN_BLOCK = 128
CONV_WIDTH = 3
DEPTH = 2
DEEPNORM_ALPHA = (2.0 * DEPTH) ** 0.25
LN_EPS = 1e-5
NEG_INF = -1e30

LANES = 128
SUBLANES = 8
MXU_DIM = 256
VMEM_LIMIT_BYTES = 56 * 1024 * 1024

DILATIONS = tuple(d for _, d in DILATED_BRANCHES)
QK_SCALE_LOG2 = math.log2(math.e) / math.sqrt(HEAD_DIM)
POOL_HALO = max(POOL_WINDOWS)

F32 = jnp.float32
BF16 = jnp.bfloat16


def _layer_norm(y, g, b):
    mu = jnp.mean(y, axis=-1, keepdims=True)
    d = y - mu
    var = jnp.mean(d * d, axis=-1, keepdims=True)
    return d * lax.rsqrt(var + LN_EPS) * g + b


def _const_spec(shape):
    nd = len(shape)
    return pl.BlockSpec(shape, lambda *_: (0,) * nd, pipeline_mode=pl.Buffered(1))


def _pool_kernel(x_ref, win_ref, wgrp_ref, scale_ref, wout_ref, g_ref, b_ref, o_ref,
                 pbuf, mbuf, *, tm, blocks_per_seq):
    i = pl.program_id(0)
    blk = i % blocks_per_seq
    d_model = x_ref.shape[1]
    gdim = d_model // len(POOL_WINDOWS)

    @pl.when(blk == 0)
    def _():
        pbuf[0:POOL_HALO, :] = jnp.zeros((POOL_HALO, d_model), F32)

    @pl.when(blk != 0)
    def _():
        pbuf[0:POOL_HALO, :] = pbuf[tm:tm + POOL_HALO, :]

    x = x_ref[...]
    p = jnp.dot(x.astype(BF16), win_ref[...], preferred_element_type=F32)
    pbuf[POOL_HALO:POOL_HALO + tm, :] = p

    pos = blk * tm + lax.broadcasted_iota(jnp.int32, (tm, 1), 0)
    for g, w in enumerate(POOL_WINDOWS):
        cols = slice(g * gdim, (g + 1) * gdim)
        s = pbuf[:, cols]
        shift = 1
        while shift < w:
            s = s + pltpu.roll(s, shift, axis=0)
            shift *= 2
        cnt = jnp.minimum(pos + 1, w).astype(F32)
        pooled = s[POOL_HALO:, :] / cnt - p[:, cols]
        mixed = jnp.dot(pooled.astype(BF16), wgrp_ref[g], preferred_element_type=F32)
        mbuf[:, cols] = (mixed * scale_ref[:, cols]).astype(BF16)

    mix = jnp.dot(mbuf[...], wout_ref[...], preferred_element_type=F32)
    o_ref[...] = _layer_norm(DEEPNORM_ALPHA * x + mix, g_ref[...], b_ref[...])


def _pool_layer(x2d, w_in, w_grp, scale, w_out, ln_g, ln_b, *, seq, tm=512):
    t, d = x2d.shape
    ng, gdim, _ = w_grp.shape
    kern = functools.partial(_pool_kernel, tm=tm, blocks_per_seq=seq // tm)
    return pl.pallas_call(
        kern,
        grid=(t // tm,),
        in_specs=[
            pl.BlockSpec((tm, d), lambda i: (i, 0)),
            _const_spec((d, d)),
            _const_spec((ng, gdim, gdim)),
            _const_spec((1, d)),
            _const_spec((d, d)),
            _const_spec((1, d)),
            _const_spec((1, d)),
        ],
        out_specs=pl.BlockSpec((tm, d), lambda i: (i, 0)),
        out_shape=jax.ShapeDtypeStruct((t, d), F32),
        scratch_shapes=[
            pltpu.VMEM((POOL_HALO + tm, d), F32),
            pltpu.VMEM((tm, d), BF16),
        ],
        compiler_params=pltpu.CompilerParams(
            dimension_semantics=("arbitrary",), vmem_limit_bytes=VMEM_LIMIT_BYTES),
        name="pool_layer",
    )(x2d, w_in, w_grp, scale, w_out, ln_g, ln_b)


def _ffn_kernel(h_ref, wg_ref, wv_ref, cp_ref, wd_ref, g_ref, b_ref, *refs,
                tm, nf, blocks_per_seq, n_out):
    out_refs = refs[:n_out]
    hb, acc, ug_buf, uv_buf, carry_g, carry_v, *act_bufs = refs[n_out:]
    i = pl.program_id(0)
    f = pl.program_id(1)
    seq_start = (i % blocks_per_seq) == 0
    tf = wg_ref.shape[1]
    fp = nf * tf

    def conv(uc, buf, carry, cp_base, c):
        cs = slice(c * LANES, (c + 1) * LANES)
        cp = cp_ref[:, pl.ds(pl.multiple_of(cp_base + f * tf + c * LANES, LANES), LANES)]
        buf[c, 0:SUBLANES, :] = jnp.where(seq_start, 0.0, carry[f, :, cs])
        buf[c, SUBLANES:SUBLANES + tm, :] = uc
        carry[f, :, cs] = uc[tm - SUBLANES:tm, :]
        out = cp[CONV_WIDTH:CONV_WIDTH + 1, :] + uc * cp[CONV_WIDTH - 1:CONV_WIDTH, :]
        for lag in range(1, CONV_WIDTH):
            tap = CONV_WIDTH - 1 - lag
            out = out + buf[c, SUBLANES - lag:SUBLANES - lag + tm, :] * cp[tap:tap + 1, :]
        return out

    def up_stage(act_dst):
        ug = jnp.dot(hb[...], wg_ref[...], preferred_element_type=F32)
        uv = jnp.dot(hb[...], wv_ref[...], preferred_element_type=F32)
        for c in range(tf // LANES):
            cs = slice(c * LANES, (c + 1) * LANES)
            gate = conv(ug[:, cs], ug_buf, carry_g, 0, c)
            val = conv(uv[:, cs], uv_buf, carry_v, fp, c)
            sig_den = 1.0 + jnp.exp2(gate * (-math.log2(math.e)))
            act_dst[:, cs] = (gate / sig_den * val).astype(BF16)

    def down_stage(act_src):
        acc[...] += jnp.dot(act_src[...], wd_ref[...], preferred_element_type=F32)

    @pl.when(f == 0)
    def _():
        hb[...] = h_ref[...].astype(BF16)
        acc[...] = jnp.zeros(acc.shape, F32)
        up_stage(act_bufs[0])

    for parity in range(2):
        @pl.when((f > 0) & (f < nf) & (f % 2 == parity))
        def _():
            up_stage(act_bufs[parity])
            down_stage(act_bufs[1 - parity])

    @pl.when(f == nf)
    def _():
        act_src = act_bufs[(nf - 1) % 2]
        half = tm // 2
        for rows in (slice(0, half), slice(half, tm)):
            mix = acc[rows, :] + jnp.dot(act_src[rows, :], wd_ref[...],
                                         preferred_element_type=F32)
            out = _layer_norm(DEEPNORM_ALPHA * h_ref[rows, :] + mix, g_ref[...], b_ref[...])
            for o_ref in out_refs:
                o_ref[rows, :] = out.astype(o_ref.dtype)


def _ffn_layer(h2d, w_gv, cp, w_down, ln_g, ln_b, *, layer, seq, bf16_copy, tm=512):
    t, d = h2d.shape
    nf, tf = w_gv.shape[2], w_gv.shape[4]
    fp = nf * tf
    chunk_spec = lambda half: pl.BlockSpec(
        (None, None, None, d, tf), lambda i, f: (layer, half, jnp.minimum(f, nf - 1), 0, 0))
    out_dtypes = (F32, BF16) if bf16_copy else (F32,)
    kern = functools.partial(_ffn_kernel, tm=tm, nf=nf, blocks_per_seq=seq // tm,
                             n_out=len(out_dtypes))
    return pl.pallas_call(
        kern,
        grid=(t // tm, nf + 1),
        in_specs=[
            pl.BlockSpec((tm, d), lambda i, f: (i, 0)),
            chunk_spec(0),
            chunk_spec(1),
            _const_spec((SUBLANES, 2 * fp)),
            pl.BlockSpec((None, tf, d), lambda i, f: (layer, jnp.maximum(f - 1, 0), 0)),
            _const_spec((1, d)),
            _const_spec((1, d)),
        ],
        out_specs=[pl.BlockSpec((tm, d), lambda i, f: (i, 0)) for _ in out_dtypes],
        out_shape=[jax.ShapeDtypeStruct((t, d), dt) for dt in out_dtypes],
        scratch_shapes=[
            pltpu.VMEM((tm, d), BF16),
            pltpu.VMEM((tm, d), F32),
            pltpu.VMEM((tf // LANES, SUBLANES + tm, LANES), F32),
            pltpu.VMEM((tf // LANES, SUBLANES + tm, LANES), F32),
            pltpu.VMEM((nf, SUBLANES, tf), F32),
            pltpu.VMEM((nf, SUBLANES, tf), F32),
            pltpu.VMEM((tm, tf), BF16),
            pltpu.VMEM((tm, tf), BF16),
        ],
        compiler_params=pltpu.CompilerParams(
            dimension_semantics=("arbitrary", "arbitrary"), vmem_limit_bytes=VMEM_LIMIT_BYTES),
        name="ffn_layer",
    )(h2d, w_gv, w_gv, cp, w_down, ln_g, ln_b)


def _lagged_block(nblocks, blocks_per_seq, lag):
    def split(g):
        j = jnp.clip(g - lag, 0, nblocks - 1)
        return j // blocks_per_seq, j % blocks_per_seq
    return split


def _nat_spec(tm, width, split):
    def idx(g):
        b, i = split(g)
        return (b, i, 0)
    return pl.BlockSpec((None, tm, width), idx)


def _dilated_spec(dil, tm, width, split):
    def idx(g):
        b, i = split(g)
        return (b, 0, i, 0)
    return pl.BlockSpec((None, dil, tm // dil, width), idx)


def _proj_kernel(x_ref, w_ref, *refs, nblocks, out_scale):
    nat_ref, o4_ref, o16_ref = refs[:3]
    ybufs, zbuf = refs[3:5], refs[5]
    n_tiles, tm, _ = ybufs[0].shape
    g = pl.program_id(0)
    tiles_per_dot = MXU_DIM // LANES

    def matmul(ybuf, xb, n):
        cols = slice(n * MXU_DIM, (n + 1) * MXU_DIM)
        y = jnp.dot(xb, w_ref[:, cols], preferred_element_type=F32)
        if out_scale != 1.0:
            y = y * out_scale
        nat_ref[0, :, cols] = y.astype(BF16)
        for t in range(tiles_per_dot):
            ybuf[n * tiles_per_dot + t] = y[:, t * LANES:(t + 1) * LANES]

    def scatter(ybuf, n):
        for c in range(n * tiles_per_dot, (n + 1) * tiles_per_dot):
            cs = slice(c * LANES, (c + 1) * LANES)
            for r in range(4):
                slab = ybuf[c, pl.ds(r, tm // 4, stride=4), :]
                o4_ref[r, :, cs] = slab.astype(BF16)
                zbuf[c, r] = slab
            for r in range(4):
                for q in range(4):
                    o16_ref[4 * q + r, :, cs] = (
                        zbuf[c, r, pl.ds(q, tm // 16, stride=4), :].astype(BF16))

    def step(y_dst, y_src):
        xb = None if y_dst is None else x_ref[...].astype(BF16)
        for n in range(n_tiles // tiles_per_dot):
            if y_dst is not None:
                matmul(y_dst, xb, n)
            if y_src is not None:
                scatter(y_src, n)

    @pl.when(g == 0)
    def _():
        step(ybufs[0], None)

    for parity in range(2):
        @pl.when((g > 0) & (g < nblocks) & (g % 2 == parity))
        def _():
            step(ybufs[parity], ybufs[1 - parity])

    @pl.when(g == nblocks)
    def _():
        step(None, ybufs[(nblocks - 1) % 2])


def _proj(h3d, w, *, out_scale=1.0, tm=512):
    bsz, seq, d = h3d.shape
    blocks_per_seq = seq // tm
    nblocks = bsz * blocks_per_seq
    cur = _lagged_block(nblocks, blocks_per_seq, 0)
    prev = _lagged_block(nblocks, blocks_per_seq, 1)
    assert DILATIONS == (1, 4, 16), "the two-step row de-interleave is written for 1, 4, 16"
    ybuf = pltpu.VMEM((d // LANES, tm, LANES), F32)
    zbuf = pltpu.VMEM((d // LANES, 4, tm // 4, LANES), F32)
    return pl.pallas_call(
        functools.partial(_proj_kernel, nblocks=nblocks, out_scale=out_scale),
        grid=(nblocks + 1,),
        in_specs=[_nat_spec(tm, d, cur), _const_spec((d, d))],
        out_specs=[_dilated_spec(1, tm, d, cur), _dilated_spec(4, tm, d, prev),
                   _dilated_spec(16, tm, d, prev)],
        out_shape=[jax.ShapeDtypeStruct((bsz, dil, seq // dil, d), BF16) for dil in DILATIONS],
        scratch_shapes=[ybuf, ybuf, zbuf],
        compiler_params=pltpu.CompilerParams(
            dimension_semantics=("arbitrary",), vmem_limit_bytes=VMEM_LIMIT_BYTES),
        name="proj",
    )(h3d, w)


def _attn_kernel(q_ref, kp_ref, kc_ref, vp_ref, vc_ref, o_ref, lse_ref, bias,
                 *, qb, n_heads, heads_per_iter):
    i = pl.program_id(2)
    nblk = qb // ATTN_BLOCK
    blk = ATTN_BLOCK

    row = lax.broadcasted_iota(jnp.int32, (blk, 2 * blk), 0)
    col = lax.broadcasted_iota(jnp.int32, (blk, 2 * blk), 1)
    dist = blk + row - col
    band = (dist >= 0) & (dist <= blk)
    has_prev = jnp.broadcast_to(i > 0, band.shape)
    bias[1] = jnp.where(band, 0.0, NEG_INF)
    bias[0] = jnp.where(band & ((col >= blk) | has_prev), 0.0, NEG_INF)

    lane = lax.broadcasted_iota(jnp.int32, (blk, LANES), 1)
    ones = jnp.ones((2 * blk, HEAD_DIM), BF16)
    lse_ref[...] = jnp.zeros(lse_ref.shape, F32)

    def one_block(h, j):
        hc = pl.ds(pl.multiple_of(h * HEAD_DIM, HEAD_DIM), HEAD_DIM)
        rows = slice(j * blk, (j + 1) * blk)
        q = q_ref[rows, hc]
        if j == 0:
            kk = jnp.concatenate([kp_ref[:, hc], kc_ref[0:blk, hc]], axis=0)
            vv = jnp.concatenate([vp_ref[:, hc], vc_ref[0:blk, hc]], axis=0)
        else:
            kk = kc_ref[(j - 1) * blk:(j + 1) * blk, hc]
            vv = vc_ref[(j - 1) * blk:(j + 1) * blk, hc]
        s = lax.dot_general(q, kk, (((1,), (1,)), ((), ())), preferred_element_type=F32)
        s = s + bias[min(j, 1)]
        m = jnp.max(s, axis=-1, keepdims=True)
        p = jnp.exp2(s - m)
        ov = jnp.dot(p.astype(BF16), jnp.concatenate([vv, ones], axis=1),
                     preferred_element_type=F32)
        den = ov[:, HEAD_DIM:]
        o_ref[rows, hc] = (ov[:, :HEAD_DIM] / den).astype(o_ref.dtype)
        lse = m + jnp.log2(den)
        lse_ref[rows, :] = jnp.where(lane == h, lse, lse_ref[rows, :])

    def head_group(g, carry):
        for hh in range(heads_per_iter):
            for j in range(nblk):
                one_block(g * heads_per_iter + hh, j)
        return carry

    lax.fori_loop(0, n_heads // heads_per_iter, head_group, 0)


def _attn_branch(q, k, v, *, qb, heads_per_iter):
    bsz, dil, length, d = q.shape
    n_heads = d // HEAD_DIM
    per = qb // ATTN_BLOCK
    cur = pl.BlockSpec((None, None, qb, d), lambda b, r, i: (b, r, i, 0))
    prev = pl.BlockSpec((None, None, ATTN_BLOCK, d),
                        lambda b, r, i: (b, r, jnp.maximum(i * per - 1, 0), 0))
    kern = functools.partial(_attn_kernel, qb=qb, n_heads=n_heads, heads_per_iter=heads_per_iter)
    return pl.pallas_call(
        kern,
        grid=(bsz, dil, length // qb),
        in_specs=[cur, prev, cur, prev, cur],
        out_specs=[cur, pl.BlockSpec((None, None, qb, LANES), lambda b, r, i: (b, r, i, 0))],
        out_shape=[
            jax.ShapeDtypeStruct((bsz, dil, length, d), BF16),
            jax.ShapeDtypeStruct((bsz, dil, length, LANES), F32),
        ],
        scratch_shapes=[pltpu.VMEM((2, ATTN_BLOCK, 2 * ATTN_BLOCK), F32)],
        compiler_params=pltpu.CompilerParams(
            dimension_semantics=("arbitrary", "arbitrary", "arbitrary"),
            vmem_limit_bytes=VMEM_LIMIT_BYTES),
        name=f"attn_d{dil}",
    )(q, k, k, v, v)


def _attn_out_kernel(o1_ref, o4_ref, o16_ref, l1_ref, l4_ref, l16_ref, h_ref, wo_ref, g_ref,
                     b_ref, out_ref, o_nat, z16, l_nat, zl16, ybuf, *xbufs, n_heads, nblocks):
    tm = h_ref.shape[0]
    g = pl.program_id(0)
    heads_per_dot = MXU_DIM // HEAD_DIM

    def to_natural(dst, src4):
        for r in range(4):
            dst[pl.ds(r, tm // 4, stride=4), :] = src4(r)

    def from_d16(tmp, src16):
        for r in range(4):
            for q in range(4):
                tmp[r, pl.ds(q, tm // 16, stride=4), :] = src16(4 * q + r)
        return lambda r: tmp[r]

    def merge_weights():
        to_natural(l_nat.at[0], lambda r: l4_ref[r])
        to_natural(l_nat.at[1], from_d16(zl16, lambda r: l16_ref[r]))
        lses = [l1_ref[0], l_nat[0], l_nat[1]]
        mx = functools.reduce(jnp.maximum, lses)
        es = [jnp.exp2(l - mx) for l in lses]
        tot = functools.reduce(lambda a, b: a + b, es)
        return es[0] / tot, es[1] / tot

    def merge_head(xbuf, w1, w4, h):
        hc = slice(h * HEAD_DIM, (h + 1) * HEAD_DIM)
        to_natural(o_nat.at[0, h], lambda r: o4_ref[r, :, hc].astype(F32))
        to_natural(o_nat.at[1, h],
                   from_d16(z16.at[h], lambda r: o16_ref[r, :, hc].astype(F32)))
        o1 = o1_ref[0, :, hc].astype(F32)
        o4 = o_nat[0, h]
        o16 = o_nat[1, h]
        shape = (tm, HEAD_DIM)
        o = (o16 + jnp.broadcast_to(w1[:, h:h + 1], shape) * (o1 - o16)
             + jnp.broadcast_to(w4[:, h:h + 1], shape) * (o4 - o16))
        xbuf[:, hc] = o.astype(BF16)

    def step(x_dst, x_src):
        if x_dst is not None:
            w1, w4 = merge_weights()
        for n in range(n_heads // heads_per_dot):
            if x_src is not None:
                cols = slice(n * MXU_DIM, (n + 1) * MXU_DIM)
                ybuf[:, cols] = jnp.dot(x_src[...], wo_ref[:, cols], preferred_element_type=F32)
            if x_dst is not None:
                for h in range(n * heads_per_dot, (n + 1) * heads_per_dot):
                    merge_head(x_dst, w1, w4, h)
        if x_src is not None:
            y = DEEPNORM_ALPHA * h_ref[...] + ybuf[...]
            out_ref[...] = _layer_norm(y, g_ref[...], b_ref[...])

    @pl.when(g == 0)
    def _():
        step(xbufs[0], None)

    for parity in range(2):
        @pl.when((g > 0) & (g < nblocks) & (g % 2 == parity))
        def _():
            step(xbufs[parity], xbufs[1 - parity])

    @pl.when(g == nblocks)
    def _():
        step(None, xbufs[(nblocks - 1) % 2])


def _attn_out(os, ls, h3d, w_o, ln_g, ln_b, *, tm=256):
    bsz, seq, d = h3d.shape
    assert DILATIONS == (1, 4, 16), "the two-step row interleave is written for 1, 4, 16"
    n_heads = d // HEAD_DIM
    blocks_per_seq = seq // tm
    nblocks = bsz * blocks_per_seq
    cur = _lagged_block(nblocks, blocks_per_seq, 0)
    prev = _lagged_block(nblocks, blocks_per_seq, 1)
    xbuf = pltpu.VMEM((tm, d), BF16)
    kern = functools.partial(_attn_out_kernel, n_heads=n_heads, nblocks=nblocks)
    return pl.pallas_call(
        kern,
        grid=(nblocks + 1,),
        in_specs=([_dilated_spec(dil, tm, d, cur) for dil in DILATIONS]
                  + [_dilated_spec(dil, tm, LANES, cur) for dil in DILATIONS]
                  + [_nat_spec(tm, d, prev), _const_spec((d, d)), _const_spec((1, d)),
                     _const_spec((1, d))]),
        out_specs=_nat_spec(tm, d, prev),
        out_shape=jax.ShapeDtypeStruct((bsz, seq, d), F32),
        scratch_shapes=[
            pltpu.VMEM((2, n_heads, tm, HEAD_DIM), F32),
            pltpu.VMEM((n_heads, 4, tm // 4, HEAD_DIM), F32),
            pltpu.VMEM((2, tm, LANES), F32),
            pltpu.VMEM((4, tm // 4, LANES), F32),
            pltpu.VMEM((tm, d), F32),
            xbuf, xbuf,
        ],
        compiler_params=pltpu.CompilerParams(
            dimension_semantics=("arbitrary",), vmem_limit_bytes=VMEM_LIMIT_BYTES),
        name="attn_out",
    )(*os, *ls, h3d, w_o, ln_g, ln_b)


def _cast_pad_kernel(x_ref, o_ref, *, axis):
    n = x_ref.shape[axis]
    head = tuple(slice(0, n) if a == axis else slice(None) for a in range(2))
    tail = tuple(slice(n, None) if a == axis else slice(None) for a in range(2))
    o_ref[head] = x_ref[...].astype(BF16)
    if o_ref.shape[axis] > n:
        o_ref[tail] = jnp.zeros(o_ref[tail].shape, BF16)


def _cast_chunks_kernel(x_ref, o_ref):
    nf, _, tf = o_ref.shape
    f = x_ref.shape[1]
    for c in range(nf):
        width = min(tf, f - c * tf)
        o_ref[c, :, :width] = x_ref[:, c * tf:c * tf + width].astype(BF16)
        if width < tf:
            o_ref[c, :, width:] = jnp.zeros((o_ref.shape[1], tf - width), BF16)


def _ffn_weights(w_up, w_down, tf, *, rows=256, cols=256):
    n_layers, d, two_f = w_up.shape
    f = two_f // 2
    fp = f + (-f % tf)
    nf = fp // tf
    params = pltpu.CompilerParams(dimension_semantics=("arbitrary",) * 3,
                                  vmem_limit_bytes=VMEM_LIMIT_BYTES)
    w_gv = pl.pallas_call(
        _cast_chunks_kernel,
        grid=(n_layers, d // rows, 2),
        in_specs=[pl.BlockSpec((None, rows, f), lambda l, r, j: (l, r, j))],
        out_specs=pl.BlockSpec((None, None, nf, rows, tf), lambda l, r, j: (l, j, 0, r, 0)),
        out_shape=jax.ShapeDtypeStruct((n_layers, 2, nf, d, tf), BF16),
        compiler_params=params,
        name="ffn_w_up_bf16",
    )(w_up)
    w_dn = pl.pallas_call(
        functools.partial(_cast_pad_kernel, axis=0),
        grid=(n_layers, d // cols, 1),
        in_specs=[pl.BlockSpec((None, f, cols), lambda l, c, _: (l, 0, c))],
        out_specs=pl.BlockSpec((None, fp, cols), lambda l, c, _: (l, 0, c)),
        out_shape=jax.ShapeDtypeStruct((n_layers, fp, d), BF16),
        compiler_params=params,
        name="ffn_w_down_bf16",
    )(w_down)
    return w_gv, w_dn, fp


def _conv_params(conv_w, conv_b, fp):
    two_f = conv_w.shape[1]
    f = two_f // 2
    fill = jnp.zeros((SUBLANES - CONV_WIDTH - 1, two_f), F32)
    cp = jnp.concatenate([conv_w, conv_b[None, :], fill], axis=0)
    z = jnp.zeros((SUBLANES, fp - f), F32)
    return jnp.concatenate([cp[:, :f], z, cp[:, f:], z], axis=1)


def kernel(x, pool_w_in, pool_w_grp, pool_scale, pool_w_out, attn_w_q, attn_w_o, shared_w_k,
           shared_w_v, ffn_w_up, ffn_conv_w, ffn_conv_b, ffn_w_down, ln1_g, ln1_b, ln2_g, ln2_b):
    bsz, seq, d = x.shape
    n_a = pool_w_in.shape[0]
    n_layers = ffn_w_up.shape[0]
    tf = 2 * MXU_DIM
    row = lambda a: a.reshape(1, d)

    w_gv, w_dn, fp = _ffn_weights(ffn_w_up, ffn_w_down, tf)
    h = x.reshape(bsz * seq, d)
    h_bf = None
    kv = None
    for i in range(n_layers):
        if i < n_a:
            h = _pool_layer(h, pool_w_in[i].astype(BF16), pool_w_grp[i].astype(BF16),
                            row(pool_scale[i]), pool_w_out[i].astype(BF16),
                            row(ln1_g[i]), row(ln1_b[i]), seq=seq)
        else:
            h3d = h.reshape(bsz, seq, d)
            x3d = h3d if h_bf is None else h_bf.reshape(bsz, seq, d)
            if kv is None:
                kv = (_proj(x3d, shared_w_k.astype(BF16)), _proj(x3d, shared_w_v.astype(BF16)))
            qs = _proj(x3d, attn_w_q[i - n_a].astype(BF16), out_scale=QK_SCALE_LOG2)
            outs = []
            for q, k, v in zip(qs, kv[0], kv[1]):
                qb = min(q.shape[2], 8 * ATTN_BLOCK)
                outs.append(_attn_branch(q, k, v, qb=qb, heads_per_iter=32 * ATTN_BLOCK // qb))
            h = _attn_out([o for o, _ in outs], [l for _, l in outs], h3d,
                          attn_w_o[i - n_a].astype(BF16), row(ln1_g[i]), row(ln1_b[i]))
            h = h.reshape(bsz * seq, d)
        cp = _conv_params(ffn_conv_w[i], ffn_conv_b[i], fp)
        feeds_attention = n_a <= i + 1 < n_layers
        outs = _ffn_layer(h, w_gv, cp, w_dn, row(ln2_g[i]), row(ln2_b[i]), layer=i, seq=seq,
                          bf16_copy=feeds_attention)
        h, h_bf = (outs[0], outs[1]) if feeds_attention else (outs[0], None)
    return h.reshape(bsz, seq, d)
```

```python
import functools
import math

import jax
import jax.numpy as jnp
from jax import lax
from jax.experimental import pallas as pl
from jax.experimental.pallas import tpu as pltpu

POOL_WINDOWS = (2, 4, 8, 16)
HEAD_DIM = 128
DILATED_BRANCHES = ((128, 1), (512, 4), (2048, 16))
ATTN_BLOCK = 128
CONV_WIDTH = 3
DEPTH = 2
DEEPNORM_ALPHA = (2.0 * DEPTH) ** 0.25
LN_EPS = 1e-5
NEG_INF = -1e30

LANES = 128
SUBLANES = 8
MXU_DIM = 256
VMEM_LIMIT_BYTES = 56 * 1024 * 1024

DILATIONS = tuple(d for _, d in DILATED_BRANCHES)
QK_SCALE_LOG2 = math.log2(math.e) / math.sqrt(HEAD_DIM)
POOL_HALO = max(POOL_WINDOWS)

F32 = jnp.float32
BF16 = jnp.bfloat16


def _layer_norm(y, g, b):
    mu = jnp.mean(y, axis=-1, keepdims=True)
    d = y - mu
    var = jnp.mean(d * d, axis=-1, keepdims=True)
    return d * lax.rsqrt(var + LN_EPS) * g + b


def _const_spec(shape):
    nd = len(shape)
    return pl.BlockSpec(shape, lambda *_: (0,) * nd, pipeline_mode=pl.Buffered(1))


def _pool_kernel(x_ref, win_ref, wgrp_ref, scale_ref, wout_ref, g_ref, b_ref, o_ref,
                 pbuf, mbuf, *, tm, blocks_per_seq):
    i = pl.program_id(0)
    blk = i % blocks_per_seq
    d_model = x_ref.shape[1]
    gdim = d_model // len(POOL_WINDOWS)

    @pl.when(blk == 0)
    def _():
        pbuf[0:POOL_HALO, :] = jnp.zeros((POOL_HALO, d_model), F32)

    @pl.when(blk != 0)
    def _():
        pbuf[0:POOL_HALO, :] = pbuf[tm:tm + POOL_HALO, :]

    x = x_ref[...]
    p = jnp.dot(x.astype(BF16), win_ref[...], preferred_element_type=F32)
    pbuf[POOL_HALO:POOL_HALO + tm, :] = p

    pos = blk * tm + lax.broadcasted_iota(jnp.int32, (tm, 1), 0)
    for g, w in enumerate(POOL_WINDOWS):
        cols = slice(g * gdim, (g + 1) * gdim)
        s = pbuf[:, cols]
        shift = 1
        while shift < w:
            s = s + pltpu.roll(s, shift, axis=0)
            shift *= 2
        cnt = jnp.minimum(pos + 1, w).astype(F32)
        pooled = s[POOL_HALO:, :] / cnt - p[:, cols]
        mixed = jnp.dot(pooled.astype(BF16), wgrp_ref[g], preferred_element_type=F32)
        mbuf[:, cols] = (mixed * scale_ref[:, cols]).astype(BF16)

    mix = jnp.dot(mbuf[...], wout_ref[...], preferred_element_type=F32)
    o_ref[...] = _layer_norm(DEEPNORM_ALPHA * x + mix, g_ref[...], b_ref[...])


def _pool_layer(x2d, w_in, w_grp, scale, w_out, ln_g, ln_b, *, seq, tm=512):
    t, d = x2d.shape
    ng, gdim, _ = w_grp.shape
    kern = functools.partial(_pool_kernel, tm=tm, blocks_per_seq=seq // tm)
    return pl.pallas_call(
        kern,
        grid=(t // tm,),
        in_specs=[
            pl.BlockSpec((tm, d), lambda i: (i, 0)),
            _const_spec((d, d)),
            _const_spec((ng, gdim, gdim)),
            _const_spec((1, d)),
            _const_spec((d, d)),
            _const_spec((1, d)),
            _const_spec((1, d)),
        ],
        out_specs=pl.BlockSpec((tm, d), lambda i: (i, 0)),
        out_shape=jax.ShapeDtypeStruct((t, d), F32),
        scratch_shapes=[
            pltpu.VMEM((POOL_HALO + tm, d), F32),
            pltpu.VMEM((tm, d), BF16),
        ],
        compiler_params=pltpu.CompilerParams(
            dimension_semantics=("arbitrary",), vmem_limit_bytes=VMEM_LIMIT_BYTES),
        name="pool_layer",
    )(x2d, w_in, w_grp, scale, w_out, ln_g, ln_b)


def _ffn_kernel(h_ref, wgv_ref, cp_ref, wd_ref, g_ref, b_ref, *refs,
                tm, nf, blocks_per_seq, n_out):
    out_refs = refs[:n_out]
    hb, acc, ug_buf, uv_buf, carry_g, carry_v, *act_bufs = refs[n_out:]
    i = pl.program_id(0)
    f = pl.program_id(1)
    seq_start = (i % blocks_per_seq) == 0
    wg_ref, wv_ref = wgv_ref.at[0], wgv_ref.at[1]
    tf = wgv_ref.shape[2]
    fp = nf * tf

    def conv(uc, buf, carry, cp_base, c):
        cs = slice(c * LANES, (c + 1) * LANES)
        cp = cp_ref[:, pl.ds(pl.multiple_of(cp_base + f * tf + c * LANES, LANES), LANES)]
        buf[c, 0:SUBLANES, :] = jnp.where(seq_start, 0.0, carry[f, :, cs])
        buf[c, SUBLANES:SUBLANES + tm, :] = uc
        carry[f, :, cs] = uc[tm - SUBLANES:tm, :]
        out = cp[CONV_WIDTH:CONV_WIDTH + 1, :] + uc * cp[CONV_WIDTH - 1:CONV_WIDTH, :]
        for lag in range(1, CONV_WIDTH):
            tap = CONV_WIDTH - 1 - lag
            out = out + buf[c, SUBLANES - lag:SUBLANES - lag + tm, :] * cp[tap:tap + 1, :]
        return out

    def up_stage(act_dst):
        ug = jnp.dot(hb[...], wg_ref[...], preferred_element_type=F32)
        uv = jnp.dot(hb[...], wv_ref[...], preferred_element_type=F32)
        for c in range(tf // LANES):
            cs = slice(c * LANES, (c + 1) * LANES)
            gate = conv(ug[:, cs], ug_buf, carry_g, 0, c)
            val = conv(uv[:, cs], uv_buf, carry_v, fp, c)
            sig_den = 1.0 + jnp.exp2(gate * (-math.log2(math.e)))
            act_dst[:, cs] = (gate / sig_den * val).astype(BF16)

    def down_stage(act_src):
        acc[...] += jnp.dot(act_src[...], wd_ref[...], preferred_element_type=F32)

    @pl.when(f == 0)
    def _():
        hb[...] = h_ref[...].astype(BF16)
        acc[...] = jnp.zeros(acc.shape, F32)
        up_stage(act_bufs[0])

    for parity in range(2):
        @pl.when((f > 0) & (f < nf) & (f % 2 == parity))
        def _():
            up_stage(act_bufs[parity])
            down_stage(act_bufs[1 - parity])

    @pl.when(f == nf)
    def _():
        act_src = act_bufs[(nf - 1) % 2]
        half = tm // 2
        for rows in (slice(0, half), slice(half, tm)):
            mix = acc[rows, :] + jnp.dot(act_src[rows, :], wd_ref[...],
                                         preferred_element_type=F32)
            out = _layer_norm(DEEPNORM_ALPHA * h_ref[rows, :] + mix, g_ref[...], b_ref[...])
            for o_ref in out_refs:
                o_ref[rows, :] = out.astype(o_ref.dtype)


def _ffn_layer(h2d, w_gv, cp, w_down, ln_g, ln_b, *, layer, seq, bf16_copy, tm=512):
    t, d = h2d.shape
    nf, tf = w_gv.shape[1], w_gv.shape[4]
    fp = nf * tf
    out_dtypes = (F32, BF16) if bf16_copy else (F32,)
    kern = functools.partial(_ffn_kernel, tm=tm, nf=nf, blocks_per_seq=seq // tm,
                             n_out=len(out_dtypes))
    return pl.pallas_call(
        kern,
        grid=(t // tm, nf + 1),
        in_specs=[
            pl.BlockSpec((tm, d), lambda i, f: (i, 0)),
            pl.BlockSpec((None, None, 2, d, tf),
                         lambda i, f: (layer, jnp.minimum(f, nf - 1), 0, 0, 0)),
            _const_spec((SUBLANES, 2 * fp)),
            pl.BlockSpec((None, tf, d), lambda i, f: (layer, jnp.maximum(f - 1, 0), 0)),
            _const_spec((1, d)),
            _const_spec((1, d)),
        ],
        out_specs=[pl.BlockSpec((tm, d), lambda i, f: (i, 0)) for _ in out_dtypes],
        out_shape=[jax.ShapeDtypeStruct((t, d), dt) for dt in out_dtypes],
        scratch_shapes=[
            pltpu.VMEM((tm, d), BF16),
            pltpu.VMEM((tm, d), F32),
            pltpu.VMEM((tf // LANES, SUBLANES + tm, LANES), F32),
            pltpu.VMEM((tf // LANES, SUBLANES + tm, LANES), F32),
            pltpu.VMEM((nf, SUBLANES, tf), F32),
            pltpu.VMEM((nf, SUBLANES, tf), F32),
            pltpu.VMEM((tm, tf), BF16),
            pltpu.VMEM((tm, tf), BF16),
        ],
        compiler_params=pltpu.CompilerParams(
            dimension_semantics=("arbitrary", "arbitrary"), vmem_limit_bytes=VMEM_LIMIT_BYTES),
        name="ffn_layer",
    )(h2d, w_gv, cp, w_down, ln_g, ln_b)


def _lagged_block(nblocks, blocks_per_seq, lag):
    def split(g):
        j = jnp.clip(g - lag, 0, nblocks - 1)
        return j // blocks_per_seq, j % blocks_per_seq
    return split


def _nat_spec(tm, width, split):
    def idx(g):
        b, i = split(g)
        return (b, i, 0)
    return pl.BlockSpec((None, tm, width), idx)


def _dilated_spec(dil, tm, width, split):
    def idx(g):
        b, i = split(g)
        return (b, 0, i, 0)
    return pl.BlockSpec((None, dil, tm // dil, width), idx)


def _proj_kernel(x_ref, w_ref, *refs, nblocks, out_scale):
    nat_ref, o4_ref, o16_ref = refs[:3]
    ybufs, zbuf = refs[3:5], refs[5]
    n_tiles, tm, _ = ybufs[0].shape
    g = pl.program_id(0)
    tiles_per_dot = MXU_DIM // LANES

    def matmul(ybuf, xb, n):
        cols = slice(n * MXU_DIM, (n + 1) * MXU_DIM)
        y = jnp.dot(xb, w_ref[:, cols], preferred_element_type=F32)
        if out_scale != 1.0:
            y = y * out_scale
        nat_ref[0, :, cols] = y.astype(BF16)
        for t in range(tiles_per_dot):
            ybuf[n * tiles_per_dot + t] = y[:, t * LANES:(t + 1) * LANES]

    def scatter(ybuf, n):
        for c in range(n * tiles_per_dot, (n + 1) * tiles_per_dot):
            cs = slice(c * LANES, (c + 1) * LANES)
            for r in range(4):
                slab = ybuf[c, pl.ds(r, tm // 4, stride=4), :]
                o4_ref[r, :, cs] = slab.astype(BF16)
                zbuf[c, r] = slab
            for r in range(4):
                for q in range(4):
                    o16_ref[4 * q + r, :, cs] = (
                        zbuf[c, r, pl.ds(q, tm // 16, stride=4), :].astype(BF16))

    def step(y_dst, y_src):
        xb = None if y_dst is None else x_ref[...].astype(BF16)
        for n in range(n_tiles // tiles_per_dot):
            if y_dst is not None:
                matmul(y_dst, xb, n)
            if y_src is not None:
                scatter(y_src, n)

    @pl.when(g == 0)
    def _():
        step(ybufs[0], None)

    for parity in range(2):
        @pl.when((g > 0) & (g < nblocks) & (g % 2 == parity))
        def _():
            step(ybufs[parity], ybufs[1 - parity])

    @pl.when(g == nblocks)
    def _():
        step(None, ybufs[(nblocks - 1) % 2])


def _proj(h3d, w, *, out_scale=1.0, tm=512):
    bsz, seq, d = h3d.shape
    blocks_per_seq = seq // tm
    nblocks = bsz * blocks_per_seq
    cur = _lagged_block(nblocks, blocks_per_seq, 0)
    prev = _lagged_block(nblocks, blocks_per_seq, 1)
    assert DILATIONS == (1, 4, 16), "the two-step row de-interleave is written for 1, 4, 16"
    ybuf = pltpu.VMEM((d // LANES, tm, LANES), F32)
    zbuf = pltpu.VMEM((d // LANES, 4, tm // 4, LANES), F32)
    return pl.pallas_call(
        functools.partial(_proj_kernel, nblocks=nblocks, out_scale=out_scale),
        grid=(nblocks + 1,),
        in_specs=[_nat_spec(tm, d, cur), _const_spec((d, d))],
        out_specs=[_dilated_spec(1, tm, d, cur), _dilated_spec(4, tm, d, prev),
                   _dilated_spec(16, tm, d, prev)],
        out_shape=[jax.ShapeDtypeStruct((bsz, dil, seq // dil, d), BF16) for dil in DILATIONS],
        scratch_shapes=[ybuf, ybuf, zbuf],
        compiler_params=pltpu.CompilerParams(
            dimension_semantics=("arbitrary",), vmem_limit_bytes=VMEM_LIMIT_BYTES),
        name="proj",
    )(h3d, w)


def _attn_kernel(q_ref, *refs, qb, n_heads, heads_per_iter, halo):
    if halo:
        kp_ref, kc_ref, vp_ref, vc_ref, o_ref, lse_ref, bias = refs
    else:
        kc_ref, vc_ref, o_ref, lse_ref, bias = refs
        kp_ref, vp_ref = kc_ref.at[0:ATTN_BLOCK], vc_ref.at[0:ATTN_BLOCK]
    i = pl.program_id(2)
    nblk = qb // ATTN_BLOCK
    blk = ATTN_BLOCK

    row = lax.broadcasted_iota(jnp.int32, (blk, 2 * blk), 0)
    col = lax.broadcasted_iota(jnp.int32, (blk, 2 * blk), 1)
    dist = blk + row - col
    band = (dist >= 0) & (dist <= blk)
    has_prev = jnp.broadcast_to(i > 0, band.shape)
    bias[1] = jnp.where(band, 0.0, NEG_INF)
    bias[0] = jnp.where(band & ((col >= blk) | has_prev), 0.0, NEG_INF)

    lane = lax.broadcasted_iota(jnp.int32, (blk, LANES), 1)
    ones = jnp.ones((2 * blk, HEAD_DIM), BF16)
    lse_ref[...] = jnp.zeros(lse_ref.shape, F32)

    def one_block(h, j):
        hc = pl.ds(pl.multiple_of(h * HEAD_DIM, HEAD_DIM), HEAD_DIM)
        rows = slice(j * blk, (j + 1) * blk)
        q = q_ref[rows, hc]
        if j == 0:
            kk = jnp.concatenate([kp_ref[:, hc], kc_ref[0:blk, hc]], axis=0)
            vv = jnp.concatenate([vp_ref[:, hc], vc_ref[0:blk, hc]], axis=0)
        else:
            kk = kc_ref[(j - 1) * blk:(j + 1) * blk, hc]
            vv = vc_ref[(j - 1) * blk:(j + 1) * blk, hc]
        s = lax.dot_general(q, kk, (((1,), (1,)), ((), ())), preferred_element_type=F32)
        s = s + bias[min(j, 1)]
        m = jnp.max(s, axis=-1, keepdims=True)
        p = jnp.exp2(s - m)
        ov = jnp.dot(p.astype(BF16), jnp.concatenate([vv, ones], axis=1),
                     preferred_element_type=F32)
        den = ov[:, HEAD_DIM:]
        o_ref[rows, hc] = (ov[:, :HEAD_DIM] / den).astype(o_ref.dtype)
        lse = m + jnp.log2(den)
        lse_ref[rows, :] = jnp.where(lane == h, lse, lse_ref[rows, :])

    def head_group(g, carry):
        for hh in range(heads_per_iter):
            for j in range(nblk):
                one_block(g * heads_per_iter + hh, j)
        return carry

    lax.fori_loop(0, n_heads // heads_per_iter, head_group, 0)


def _attn_branch(q, k, v, *, qb, heads_per_iter):
    bsz, dil, length, d = q.shape
    n_heads = d // HEAD_DIM
    per = qb // ATTN_BLOCK
    cur = pl.BlockSpec((None, None, qb, d), lambda b, r, i: (b, r, i, 0))
    prev = pl.BlockSpec((None, None, ATTN_BLOCK, d),
                        lambda b, r, i: (b, r, jnp.maximum(i * per - 1, 0), 0))
    halo = length > qb
    kern = functools.partial(_attn_kernel, qb=qb, n_heads=n_heads, heads_per_iter=heads_per_iter,
                             halo=halo)
    return pl.pallas_call(
        kern,
        grid=(bsz, dil, length // qb),
        in_specs=[cur, prev, cur, prev, cur] if halo else [cur, cur, cur],
        out_specs=[cur, pl.BlockSpec((None, None, qb, LANES), lambda b, r, i: (b, r, i, 0))],
        out_shape=[
            jax.ShapeDtypeStruct((bsz, dil, length, d), BF16),
            jax.ShapeDtypeStruct((bsz, dil, length, LANES), F32),
        ],
        scratch_shapes=[pltpu.VMEM((2, ATTN_BLOCK, 2 * ATTN_BLOCK), F32)],
        compiler_params=pltpu.CompilerParams(
            dimension_semantics=("arbitrary", "arbitrary", "arbitrary"),
            vmem_limit_bytes=VMEM_LIMIT_BYTES),
        name=f"attn_d{dil}",
    )(*((q, k, k, v, v) if halo else (q, k, v)))


def _attn_out_kernel(o1_ref, o4_ref, o16_ref, l1_ref, l4_ref, l16_ref, h_ref, wo_ref, g_ref,
                     b_ref, out_ref, o_nat, z16, l_nat, zl16, ybuf, *xbufs, n_heads, nblocks):
    tm = h_ref.shape[0]
    g = pl.program_id(0)
    heads_per_dot = MXU_DIM // HEAD_DIM

    def to_natural(dst, src4):
        for r in range(4):
            dst[pl.ds(r, tm // 4, stride=4), :] = src4(r)

    def from_d16(tmp, src16):
        for r in range(4):
            for q in range(4):
                tmp[r, pl.ds(q, tm // 16, stride=4), :] = src16(4 * q + r)
        return lambda r: tmp[r]

    def merge_weights():
        to_natural(l_nat.at[0], lambda r: l4_ref[r])
        to_natural(l_nat.at[1], from_d16(zl16, lambda r: l16_ref[r]))
        lses = [l1_ref[0], l_nat[0], l_nat[1]]
        mx = functools.reduce(jnp.maximum, lses)
        es = [jnp.exp2(l - mx) for l in lses]
        tot = functools.reduce(lambda a, b: a + b, es)
        return es[0] / tot, es[1] / tot

    def merge_head(xbuf, w1, w4, h):
        hc = slice(h * HEAD_DIM, (h + 1) * HEAD_DIM)
        to_natural(o_nat.at[0, h], lambda r: o4_ref[r, :, hc].astype(F32))
        to_natural(o_nat.at[1, h],
                   from_d16(z16.at[h], lambda r: o16_ref[r, :, hc].astype(F32)))
        o1 = o1_ref[0, :, hc].astype(F32)
        o4 = o_nat[0, h]
        o16 = o_nat[1, h]
        shape = (tm, HEAD_DIM)
        o = (o16 + jnp.broadcast_to(w1[:, h:h + 1], shape) * (o1 - o16)
             + jnp.broadcast_to(w4[:, h:h + 1], shape) * (o4 - o16))
        xbuf[:, hc] = o.astype(BF16)

    def step(x_dst, x_src):
        if x_dst is not None:
            w1, w4 = merge_weights()
        for n in range(n_heads // heads_per_dot):
            if x_src is not None:
                cols = slice(n * MXU_DIM, (n + 1) * MXU_DIM)
                ybuf[:, cols] = jnp.dot(x_src[...], wo_ref[:, cols], preferred_element_type=F32)
            if x_dst is not None:
                for h in range(n * heads_per_dot, (n + 1) * heads_per_dot):
                    merge_head(x_dst, w1, w4, h)
        if x_src is not None:
            y = DEEPNORM_ALPHA * h_ref[...] + ybuf[...]
            out_ref[...] = _layer_norm(y, g_ref[...], b_ref[...])

    @pl.when(g == 0)
    def _():
        step(xbufs[0], None)

    for parity in range(2):
        @pl.when((g > 0) & (g < nblocks) & (g % 2 == parity))
        def _():
            step(xbufs[parity], xbufs[1 - parity])

    @pl.when(g == nblocks)
    def _():
        step(None, xbufs[(nblocks - 1) % 2])


def _attn_out(os, ls, h3d, w_o, ln_g, ln_b, *, tm=256):
    bsz, seq, d = h3d.shape
    assert DILATIONS == (1, 4, 16), "the two-step row interleave is written for 1, 4, 16"
    n_heads = d // HEAD_DIM
    blocks_per_seq = seq // tm
    nblocks = bsz * blocks_per_seq
    cur = _lagged_block(nblocks, blocks_per_seq, 0)
    prev = _lagged_block(nblocks, blocks_per_seq, 1)
    xbuf = pltpu.VMEM((tm, d), BF16)
    kern = functools.partial(_attn_out_kernel, n_heads=n_heads, nblocks=nblocks)
    return pl.pallas_call(
        kern,
        grid=(nblocks + 1,),
        in_specs=([_dilated_spec(dil, tm, d, cur) for dil in DILATIONS]
                  + [_dilated_spec(dil, tm, LANES, cur) for dil in DILATIONS]
                  + [_nat_spec(tm, d, prev), _const_spec((d, d)), _const_spec((1, d)),
                     _const_spec((1, d))]),
        out_specs=_nat_spec(tm, d, prev),
        out_shape=jax.ShapeDtypeStruct((bsz, seq, d), F32),
        scratch_shapes=[
            pltpu.VMEM((2, n_heads, tm, HEAD_DIM), F32),
            pltpu.VMEM((n_heads, 4, tm // 4, HEAD_DIM), F32),
            pltpu.VMEM((2, tm, LANES), F32),
            pltpu.VMEM((4, tm // 4, LANES), F32),
            pltpu.VMEM((tm, d), F32),
            xbuf, xbuf,
        ],
        compiler_params=pltpu.CompilerParams(
            dimension_semantics=("arbitrary",), vmem_limit_bytes=VMEM_LIMIT_BYTES),
        name="attn_out",
    )(*os, *ls, h3d, w_o, ln_g, ln_b)


def _cast_pad_kernel(x_ref, o_ref, *, axis):
    n = x_ref.shape[axis]
    head = tuple(slice(0, n) if a == axis else slice(None) for a in range(2))
    tail = tuple(slice(n, None) if a == axis else slice(None) for a in range(2))
    o_ref[head] = x_ref[...].astype(BF16)
    if o_ref.shape[axis] > n:
        o_ref[tail] = jnp.zeros(o_ref[tail].shape, BF16)


def _cast_chunks_kernel(x_ref, o_ref):
    nf, _, tf = o_ref.shape
    f = x_ref.shape[1]
    for c in range(nf):
        width = min(tf, f - c * tf)
        o_ref[c, :, :width] = x_ref[:, c * tf:c * tf + width].astype(BF16)
        if width < tf:
            o_ref[c, :, width:] = jnp.zeros((o_ref.shape[1], tf - width), BF16)


def _ffn_weights(w_up, w_down, tf, *, rows=256, cols=256):
    n_layers, d, two_f = w_up.shape
    f = two_f // 2
    fp = f + (-f % tf)
    nf = fp // tf
    params = pltpu.CompilerParams(dimension_semantics=("arbitrary",) * 3,
                                  vmem_limit_bytes=VMEM_LIMIT_BYTES)
    w_gv = pl.pallas_call(
        _cast_chunks_kernel,
        grid=(n_layers, d // rows, 2),
        in_specs=[pl.BlockSpec((None, rows, f), lambda l, r, j: (l, r, j))],
        out_specs=pl.BlockSpec((None, nf, None, rows, tf), lambda l, r, j: (l, 0, j, r, 0)),
        out_shape=jax.ShapeDtypeStruct((n_layers, nf, 2, d, tf), BF16),
        compiler_params=params,
        name="ffn_w_up_bf16",
    )(w_up)
    w_dn = pl.pallas_call(
        functools.partial(_cast_pad_kernel, axis=0),
        grid=(n_layers, d // cols, 1),
        in_specs=[pl.BlockSpec((None, f, cols), lambda l, c, _: (l, 0, c))],
        out_specs=pl.BlockSpec((None, fp, cols), lambda l, c, _: (l, 0, c)),
        out_shape=jax.ShapeDtypeStruct((n_layers, fp, d), BF16),
        compiler_params=params,
        name="ffn_w_down_bf16",
    )(w_down)
    return w_gv, w_dn, fp


def _conv_params(conv_w, conv_b, fp):
    two_f = conv_w.shape[1]
    f = two_f // 2
    fill = jnp.zeros((SUBLANES - CONV_WIDTH - 1, two_f), F32)
    cp = jnp.concatenate([conv_w, conv_b[None, :], fill], axis=0)
    z = jnp.zeros((SUBLANES, fp - f), F32)
    return jnp.concatenate([cp[:, :f], z, cp[:, f:], z], axis=1)


def kernel(x, pool_w_in, pool_w_grp, pool_scale, pool_w_out, attn_w_q, attn_w_o, shared_w_k,
           shared_w_v, ffn_w_up, ffn_conv_w, ffn_conv_b, ffn_w_down, ln1_g, ln1_b, ln2_g, ln2_b):
    bsz, seq, d = x.shape
    n_a = pool_w_in.shape[0]
    n_layers = ffn_w_up.shape[0]
    tf = 2 * MXU_DIM
    row = lambda a: a.reshape(1, d)

    w_gv, w_dn, fp = _ffn_weights(ffn_w_up, ffn_w_down, tf)
    h = x.reshape(bsz * seq, d)
    h_bf = None
    kv = None
    for i in range(n_layers):
        if i < n_a:
            h = _pool_layer(h, pool_w_in[i].astype(BF16), pool_w_grp[i].astype(BF16),
                            row(pool_scale[i]), pool_w_out[i].astype(BF16),
                            row(ln1_g[i]), row(ln1_b[i]), seq=seq)
        else:
            h3d = h.reshape(bsz, seq, d)
            x3d = h3d if h_bf is None else h_bf.reshape(bsz, seq, d)
            if kv is None:
                kv = (_proj(x3d, shared_w_k.astype(BF16)), _proj(x3d, shared_w_v.astype(BF16)))
            qs = _proj(x3d, attn_w_q[i - n_a].astype(BF16), out_scale=QK_SCALE_LOG2)
            outs = []
            for q, k, v in zip(qs, kv[0], kv[1]):
                qb = min(q.shape[2], 8 * ATTN_BLOCK)
                outs.append(_attn_branch(q, k, v, qb=qb, heads_per_iter=32 * ATTN_BLOCK // qb))
            h = _attn_out([o for o, _ in outs], [l for _, l in outs], h3d,
                          attn_w_o[i - n_a].astype(BF16), row(ln1_g[i]), row(ln1_b[i]))
            h = h.reshape(bsz * seq, d)
        cp = _conv_params(ffn_conv_w[i], ffn_conv_b[i], fp)
        feeds_attention = n_a <= i + 1 < n_layers
        outs = _ffn_layer(h, w_gv, cp, w_dn, row(ln2_g[i]), row(ln2_b[i]), layer=i, seq=seq,
                          bf16_copy=feeds_attention)
        h, h_bf = (outs[0], outs[1]) if feeds_attention else (outs[0], None)
    return h.reshape(bsz, seq, d)
```

```python
import functools
import math

import jax
import jax.numpy as jnp
from jax import lax
from jax.experimental import pallas as pl
from jax.experimental.pallas import tpu as pltpu

POOL_WINDOWS = (2, 4, 8, 16)
HEAD_DIM = 128
DILATED_BRANCHES = ((128, 1), (512, 4), (2048, 16))
ATTN_BLOCK = 128
CONV_WIDTH = 3
DEPTH = 2
DEEPNORM_ALPHA = (2.0 * DEPTH) ** 0.25
LN_EPS = 1e-5
NEG_INF = -1e30

LANES = 128
SUBLANES = 8
MXU_DIM = 256
VMEM_LIMIT_BYTES = 56 * 1024 * 1024

DILATIONS = tuple(d for _, d in DILATED_BRANCHES)
ATTN_ROWS_PER_STEP = 8 * ATTN_BLOCK
ATTN_BLOCKS_PER_ITER = 32
QK_SCALE_LOG2 = math.log2(math.e) / math.sqrt(HEAD_DIM)
POOL_HALO = max(POOL_WINDOWS)

F32 = jnp.float32
BF16 = jnp.bfloat16


def _layer_norm(y, g, b):
    mu = jnp.mean(y, axis=-1, keepdims=True)
    d = y - mu
    var = jnp.mean(d * d, axis=-1, keepdims=True)
    return d * lax.rsqrt(var + LN_EPS) * g + b


def _const_spec(shape):
    nd = len(shape)
    return pl.BlockSpec(shape, lambda *_: (0,) * nd, pipeline_mode=pl.Buffered(1))


def _pool_kernel(x_ref, win_ref, wgrp_ref, scale_ref, wout_ref, g_ref, b_ref, o_ref,
                 pbuf, mbuf, *, tm, blocks_per_seq):
    i = pl.program_id(0)
    blk = i % blocks_per_seq
    d_model = x_ref.shape[1]
    gdim = d_model // len(POOL_WINDOWS)

    @pl.when(blk == 0)
    def _():
        pbuf[0:POOL_HALO, :] = jnp.zeros((POOL_HALO, d_model), F32)

    @pl.when(blk != 0)
    def _():
        pbuf[0:POOL_HALO, :] = pbuf[tm:tm + POOL_HALO, :]

    x = x_ref[...]
    p = jnp.dot(x.astype(BF16), win_ref[...], preferred_element_type=F32)
    pbuf[POOL_HALO:POOL_HALO + tm, :] = p

    pos = blk * tm + lax.broadcasted_iota(jnp.int32, (tm, 1), 0)
    for g, w in enumerate(POOL_WINDOWS):
        cols = slice(g * gdim, (g + 1) * gdim)
        s = pbuf[:, cols]
        shift = 1
        while shift < w:
            s = s + pltpu.roll(s, shift, axis=0)
            shift *= 2
        cnt = jnp.minimum(pos + 1, w).astype(F32)
        pooled = s[POOL_HALO:, :] / cnt - p[:, cols]
        mixed = jnp.dot(pooled.astype(BF16), wgrp_ref[g], preferred_element_type=F32)
        mbuf[:, cols] = (mixed * scale_ref[:, cols]).astype(BF16)

    mix = jnp.dot(mbuf[...], wout_ref[...], preferred_element_type=F32)
    o_ref[...] = _layer_norm(DEEPNORM_ALPHA * x + mix, g_ref[...], b_ref[...])


def _pool_layer(x2d, w_in, w_grp, scale, w_out, ln_g, ln_b, *, seq, tm=512):
    t, d = x2d.shape
    ng, gdim, _ = w_grp.shape
    kern = functools.partial(_pool_kernel, tm=tm, blocks_per_seq=seq // tm)
    return pl.pallas_call(
        kern,
        grid=(t // tm,),
        in_specs=[
            pl.BlockSpec((tm, d), lambda i: (i, 0)),
            _const_spec((d, d)),
            _const_spec((ng, gdim, gdim)),
            _const_spec((1, d)),
            _const_spec((d, d)),
            _const_spec((1, d)),
            _const_spec((1, d)),
        ],
        out_specs=pl.BlockSpec((tm, d), lambda i: (i, 0)),
        out_shape=jax.ShapeDtypeStruct((t, d), F32),
        scratch_shapes=[
            pltpu.VMEM((POOL_HALO + tm, d), F32),
            pltpu.VMEM((tm, d), BF16),
        ],
        compiler_params=pltpu.CompilerParams(
            dimension_semantics=("arbitrary",), vmem_limit_bytes=VMEM_LIMIT_BYTES),
        name="pool_layer",
    )(x2d, w_in, w_grp, scale, w_out, ln_g, ln_b)


def _ffn_kernel(h_ref, wgv_ref, cp_ref, wd_ref, g_ref, b_ref, *refs,
                tm, nf, blocks_per_seq, n_out):
    out_refs = refs[:n_out]
    hb, acc, ug_buf, uv_buf, carry_g, carry_v, *act_bufs = refs[n_out:]
    i = pl.program_id(0)
    f = pl.program_id(1)
    seq_start = (i % blocks_per_seq) == 0
    wg_ref, wv_ref = wgv_ref.at[0], wgv_ref.at[1]
    tf = wgv_ref.shape[2]
    fp = nf * tf

    def conv(uc, buf, carry, cp_base, c):
        cs = slice(c * LANES, (c + 1) * LANES)
        cp = cp_ref[:, pl.ds(pl.multiple_of(cp_base + f * tf + c * LANES, LANES), LANES)]
        buf[c, 0:SUBLANES, :] = jnp.where(seq_start, 0.0, carry[f, :, cs])
        buf[c, SUBLANES:SUBLANES + tm, :] = uc
        carry[f, :, cs] = uc[tm - SUBLANES:tm, :]
        out = cp[CONV_WIDTH:CONV_WIDTH + 1, :] + uc * cp[CONV_WIDTH - 1:CONV_WIDTH, :]
        for lag in range(1, CONV_WIDTH):
            tap = CONV_WIDTH - 1 - lag
            out = out + buf[c, SUBLANES - lag:SUBLANES - lag + tm, :] * cp[tap:tap + 1, :]
        return out

    def up_stage(act_dst):
        ug = jnp.dot(hb[...], wg_ref[...], preferred_element_type=F32)
        uv = jnp.dot(hb[...], wv_ref[...], preferred_element_type=F32)
        for c in range(tf // LANES):
            cs = slice(c * LANES, (c + 1) * LANES)
            gate = conv(ug[:, cs], ug_buf, carry_g, 0, c)
            val = conv(uv[:, cs], uv_buf, carry_v, fp, c)
            sig_den = 1.0 + jnp.exp2(gate * (-math.log2(math.e)))
            act_dst[:, cs] = (gate / sig_den * val).astype(BF16)

    def down_stage(act_src):
        acc[...] += jnp.dot(act_src[...], wd_ref[...], preferred_element_type=F32)

    @pl.when(f == 0)
    def _():
        hb[...] = h_ref[...].astype(BF16)
        acc[...] = jnp.zeros(acc.shape, F32)
        up_stage(act_bufs[0])

    for parity in range(2):
        @pl.when((f > 0) & (f < nf) & (f % 2 == parity))
        def _():
            up_stage(act_bufs[parity])
            down_stage(act_bufs[1 - parity])

    @pl.when(f == nf)
    def _():
        act_src = act_bufs[(nf - 1) % 2]
        half = tm // 2
        for rows in (slice(0, half), slice(half, tm)):
            mix = acc[rows, :] + jnp.dot(act_src[rows, :], wd_ref[...],
                                         preferred_element_type=F32)
            out = _layer_norm(DEEPNORM_ALPHA * h_ref[rows, :] + mix, g_ref[...], b_ref[...])
            for o_ref in out_refs:
                o_ref[rows, :] = out.astype(o_ref.dtype)


def _ffn_layer(h2d, w_gv, cp, w_down, ln_g, ln_b, *, layer, seq, bf16_copy, tm=512):
    t, d = h2d.shape
    nf, tf = w_gv.shape[1], w_gv.shape[4]
    fp = nf * tf
    out_dtypes = (F32, BF16) if bf16_copy else (F32,)
    kern = functools.partial(_ffn_kernel, tm=tm, nf=nf, blocks_per_seq=seq // tm,
                             n_out=len(out_dtypes))
    return pl.pallas_call(
        kern,
        grid=(t // tm, nf + 1),
        in_specs=[
            pl.BlockSpec((tm, d), lambda i, f: (i, 0)),
            pl.BlockSpec((None, None, 2, d, tf),
                         lambda i, f: (layer, jnp.minimum(f, nf - 1), 0, 0, 0)),
            _const_spec((SUBLANES, 2 * fp)),
            pl.BlockSpec((None, tf, d), lambda i, f: (layer, jnp.maximum(f - 1, 0), 0)),
            _const_spec((1, d)),
            _const_spec((1, d)),
        ],
        out_specs=[pl.BlockSpec((tm, d), lambda i, f: (i, 0)) for _ in out_dtypes],
        out_shape=[jax.ShapeDtypeStruct((t, d), dt) for dt in out_dtypes],
        scratch_shapes=[
            pltpu.VMEM((tm, d), BF16),
            pltpu.VMEM((tm, d), F32),
            pltpu.VMEM((tf // LANES, SUBLANES + tm, LANES), F32),
            pltpu.VMEM((tf // LANES, SUBLANES + tm, LANES), F32),
            pltpu.VMEM((nf, SUBLANES, tf), F32),
            pltpu.VMEM((nf, SUBLANES, tf), F32),
            pltpu.VMEM((tm, tf), BF16),
            pltpu.VMEM((tm, tf), BF16),
        ],
        compiler_params=pltpu.CompilerParams(
            dimension_semantics=("arbitrary", "arbitrary"), vmem_limit_bytes=VMEM_LIMIT_BYTES),
        name="ffn_layer",
    )(h2d, w_gv, cp, w_down, ln_g, ln_b)


def _lagged_block(nblocks, blocks_per_seq, lag):
    def split(g):
        j = jnp.clip(g - lag, 0, nblocks - 1)
        return j // blocks_per_seq, j % blocks_per_seq
    return split


def _nat_spec(tm, width, split):
    def idx(g):
        b, i = split(g)
        return (b, i, 0)
    return pl.BlockSpec((None, tm, width), idx)


def _dilated_spec(dil, tm, width, split):
    def idx(g):
        b, i = split(g)
        return (b, 0, i, 0)
    return pl.BlockSpec((None, dil, tm // dil, width), idx)


def _proj_kernel(x_ref, w_ref, *refs, nblocks, out_scale):
    nat_ref, o4_ref, o16_ref = refs[:3]
    ybufs, zbuf = refs[3:5], refs[5]
    n_tiles, tm, _ = ybufs[0].shape
    g = pl.program_id(0)
    tiles_per_dot = MXU_DIM // LANES

    def matmul(ybuf, xb, n):
        cols = slice(n * MXU_DIM, (n + 1) * MXU_DIM)
        y = jnp.dot(xb, w_ref[:, cols], preferred_element_type=F32)
        if out_scale != 1.0:
            y = y * out_scale
        nat_ref[0, :, cols] = y.astype(BF16)
        for t in range(tiles_per_dot):
            ybuf[n * tiles_per_dot + t] = y[:, t * LANES:(t + 1) * LANES]

    def scatter(ybuf, n):
        for c in range(n * tiles_per_dot, (n + 1) * tiles_per_dot):
            cs = slice(c * LANES, (c + 1) * LANES)
            for r in range(4):
                slab = ybuf[c, pl.ds(r, tm // 4, stride=4), :]
                o4_ref[r, :, cs] = slab.astype(BF16)
                zbuf[c, r] = slab
            for r in range(4):
                for q in range(4):
                    o16_ref[4 * q + r, :, cs] = (
                        zbuf[c, r, pl.ds(q, tm // 16, stride=4), :].astype(BF16))

    def step(y_dst, y_src):
        xb = None if y_dst is None else x_ref[...].astype(BF16)
        for n in range(n_tiles // tiles_per_dot):
            if y_dst is not None:
                matmul(y_dst, xb, n)
            if y_src is not None:
                scatter(y_src, n)

    @pl.when(g == 0)
    def _():
        step(ybufs[0], None)

    for parity in range(2):
        @pl.when((g > 0) & (g < nblocks) & (g % 2 == parity))
        def _():
            step(ybufs[parity], ybufs[1 - parity])

    @pl.when(g == nblocks)
    def _():
        step(None, ybufs[(nblocks - 1) % 2])


def _proj(h3d, w, *, out_scale=1.0, tm=512):
    bsz, seq, d = h3d.shape
    blocks_per_seq = seq // tm
    nblocks = bsz * blocks_per_seq
    cur = _lagged_block(nblocks, blocks_per_seq, 0)
    prev = _lagged_block(nblocks, blocks_per_seq, 1)
    assert DILATIONS == (1, 4, 16), "the two-step row de-interleave is written for 1, 4, 16"
    ybuf = pltpu.VMEM((d // LANES, tm, LANES), F32)
    zbuf = pltpu.VMEM((d // LANES, 4, tm // 4, LANES), F32)
    return pl.pallas_call(
        functools.partial(_proj_kernel, nblocks=nblocks, out_scale=out_scale),
        grid=(nblocks + 1,),
        in_specs=[_nat_spec(tm, d, cur), _const_spec((d, d))],
        out_specs=[_dilated_spec(1, tm, d, cur), _dilated_spec(4, tm, d, prev),
                   _dilated_spec(16, tm, d, prev)],
        out_shape=[jax.ShapeDtypeStruct((bsz, dil, seq // dil, d), BF16) for dil in DILATIONS],
        scratch_shapes=[ybuf, ybuf, zbuf],
        compiler_params=pltpu.CompilerParams(
            dimension_semantics=("arbitrary",), vmem_limit_bytes=VMEM_LIMIT_BYTES),
        name="proj",
    )(h3d, w)


def _attn_kernel(q_ref, *refs, qb, n_sub, n_heads, heads_per_iter, halo):
    if halo:
        kp_ref, kc_ref, vp_ref, vc_ref, o_ref, lse_ref, bias = refs
    else:
        kc_ref, vc_ref, o_ref, lse_ref, bias = refs
    i = pl.program_id(2)
    nblk = qb // ATTN_BLOCK
    blk = ATTN_BLOCK

    row = lax.broadcasted_iota(jnp.int32, (blk, 2 * blk), 0)
    col = lax.broadcasted_iota(jnp.int32, (blk, 2 * blk), 1)
    dist = blk + row - col
    band = (dist >= 0) & (dist <= blk)
    has_prev = jnp.broadcast_to(i > 0, band.shape)
    bias[1] = jnp.where(band, 0.0, NEG_INF)
    bias[0] = jnp.where(band & ((col >= blk) | has_prev), 0.0, NEG_INF)

    lane = lax.broadcasted_iota(jnp.int32, (blk, LANES), 1)
    ones = jnp.ones((2 * blk, HEAD_DIM), BF16)
    lse_ref[...] = jnp.zeros(lse_ref.shape, F32)

    def one_block(sub, h, j):
        hc = pl.ds(pl.multiple_of(h * HEAD_DIM, HEAD_DIM), HEAD_DIM)
        rows = slice(j * blk, (j + 1) * blk)
        q = q_ref[sub, rows, hc]
        if j > 0:
            kk = kc_ref[sub, (j - 1) * blk:(j + 1) * blk, hc]
            vv = vc_ref[sub, (j - 1) * blk:(j + 1) * blk, hc]
        else:
            k_prev = kp_ref[:, hc] if halo else kc_ref[sub, 0:blk, hc]
            v_prev = vp_ref[:, hc] if halo else vc_ref[sub, 0:blk, hc]
            kk = jnp.concatenate([k_prev, kc_ref[sub, 0:blk, hc]], axis=0)
            vv = jnp.concatenate([v_prev, vc_ref[sub, 0:blk, hc]], axis=0)
        s = lax.dot_general(q, kk, (((1,), (1,)), ((), ())), preferred_element_type=F32)
        s = s + bias[min(j, 1)]
        m = jnp.max(s, axis=-1, keepdims=True)
        p = jnp.exp2(s - m)
        ov = jnp.dot(p.astype(BF16), jnp.concatenate([vv, ones], axis=1),
                     preferred_element_type=F32)
        den = ov[:, HEAD_DIM:]
        o_ref[sub, rows, hc] = (ov[:, :HEAD_DIM] / den).astype(o_ref.dtype)
        lse = m + jnp.log2(den)
        lse_ref[sub, rows, :] = jnp.where(lane == h, lse, lse_ref[sub, rows, :])

    def head_group(g, carry):
        for hh in range(heads_per_iter):
            for sub in range(n_sub):
                for j in range(nblk):
                    one_block(sub, g * heads_per_iter + hh, j)
        return carry

    lax.fori_loop(0, n_heads // heads_per_iter, head_group, 0)


def _attn_branch(q, k, v):
    bsz, dil, length, d = q.shape
    n_heads = d // HEAD_DIM
    qb = min(length, ATTN_ROWS_PER_STEP)
    halo = length > qb
    n_sub = 1 if halo else min(dil, ATTN_ROWS_PER_STEP // qb)
    per = qb // ATTN_BLOCK
    heads_per_iter = max(1, ATTN_BLOCKS_PER_ITER // (per * n_sub))
    cur = lambda width: pl.BlockSpec((None, n_sub, qb, width), lambda b, r, i: (b, r, i, 0))
    prev = pl.BlockSpec((None, None, ATTN_BLOCK, d),
                        lambda b, r, i: (b, r, jnp.maximum(i * per - 1, 0), 0))
    kern = functools.partial(_attn_kernel, qb=qb, n_sub=n_sub, n_heads=n_heads,
                             heads_per_iter=heads_per_iter, halo=halo)
    return pl.pallas_call(
        kern,
        grid=(bsz, dil // n_sub, length // qb),
        in_specs=[cur(d), prev, cur(d), prev, cur(d)] if halo else [cur(d)] * 3,
        out_specs=[cur(d), cur(LANES)],
        out_shape=[
            jax.ShapeDtypeStruct((bsz, dil, length, d), BF16),
            jax.ShapeDtypeStruct((bsz, dil, length, LANES), F32),
        ],
        scratch_shapes=[pltpu.VMEM((2, ATTN_BLOCK, 2 * ATTN_BLOCK), F32)],
        compiler_params=pltpu.CompilerParams(
            dimension_semantics=("arbitrary", "arbitrary", "arbitrary"),
            vmem_limit_bytes=VMEM_LIMIT_BYTES),
        name=f"attn_d{dil}",
    )(*((q, k, k, v, v) if halo else (q, k, v)))


def _attn_out_kernel(o1_ref, o4_ref, o16_ref, l1_ref, l4_ref, l16_ref, h_ref, wo_ref, g_ref,
                     b_ref, out_ref, o_nat, z16, l_nat, zl16, ybuf, *xbufs, n_heads, nblocks):
    tm = h_ref.shape[0]
    g = pl.program_id(0)
    heads_per_dot = MXU_DIM // HEAD_DIM

    def to_natural(dst, src4):
        for r in range(4):
            dst[pl.ds(r, tm // 4, stride=4), :] = src4(r)

    def from_d16(tmp, src16):
        for r in range(4):
            for q in range(4):
                tmp[r, pl.ds(q, tm // 16, stride=4), :] = src16(4 * q + r)
        return lambda r: tmp[r]

    def merge_weights():
        to_natural(l_nat.at[0], lambda r: l4_ref[r])
        to_natural(l_nat.at[1], from_d16(zl16, lambda r: l16_ref[r]))
        lses = [l1_ref[0], l_nat[0], l_nat[1]]
        mx = functools.reduce(jnp.maximum, lses)
        es = [jnp.exp2(l - mx) for l in lses]
        tot = functools.reduce(lambda a, b: a + b, es)
        return es[0] / tot, es[1] / tot

    def merge_head(xbuf, w1, w4, h):
        hc = slice(h * HEAD_DIM, (h + 1) * HEAD_DIM)
        to_natural(o_nat.at[0, h], lambda r: o4_ref[r, :, hc].astype(F32))
        to_natural(o_nat.at[1, h],
                   from_d16(z16.at[h], lambda r: o16_ref[r, :, hc].astype(F32)))
        o1 = o1_ref[0, :, hc].astype(F32)
        o4 = o_nat[0, h]
        o16 = o_nat[1, h]
        shape = (tm, HEAD_DIM)
        o = (o16 + jnp.broadcast_to(w1[:, h:h + 1], shape) * (o1 - o16)
             + jnp.broadcast_to(w4[:, h:h + 1], shape) * (o4 - o16))
        xbuf[:, hc] = o.astype(BF16)

    def step(x_dst, x_src):
        if x_dst is not None:
            w1, w4 = merge_weights()
        for n in range(n_heads // heads_per_dot):
            if x_src is not None:
                cols = slice(n * MXU_DIM, (n + 1) * MXU_DIM)
                ybuf[:, cols] = jnp.dot(x_src[...], wo_ref[:, cols], preferred_element_type=F32)
            if x_dst is not None:
                for h in range(n * heads_per_dot, (n + 1) * heads_per_dot):
                    merge_head(x_dst, w1, w4, h)
        if x_src is not None:
            y = DEEPNORM_ALPHA * h_ref[...] + ybuf[...]
            out_ref[...] = _layer_norm(y, g_ref[...], b_ref[...])

    @pl.when(g == 0)
    def _():
        step(xbufs[0], None)

    for parity in range(2):
        @pl.when((g > 0) & (g < nblocks) & (g % 2 == parity))
        def _():
            step(xbufs[parity], xbufs[1 - parity])

    @pl.when(g == nblocks)
    def _():
        step(None, xbufs[(nblocks - 1) % 2])


def _attn_out(os, ls, h3d, w_o, ln_g, ln_b, *, tm=256):
    bsz, seq, d = h3d.shape
    assert DILATIONS == (1, 4, 16), "the two-step row interleave is written for 1, 4, 16"
    n_heads = d // HEAD_DIM
    blocks_per_seq = seq // tm
    nblocks = bsz * blocks_per_seq
    cur = _lagged_block(nblocks, blocks_per_seq, 0)
    prev = _lagged_block(nblocks, blocks_per_seq, 1)
    xbuf = pltpu.VMEM((tm, d), BF16)
    kern = functools.partial(_attn_out_kernel, n_heads=n_heads, nblocks=nblocks)
    return pl.pallas_call(
        kern,
        grid=(nblocks + 1,),
        in_specs=([_dilated_spec(dil, tm, d, cur) for dil in DILATIONS]
                  + [_dilated_spec(dil, tm, LANES, cur) for dil in DILATIONS]
                  + [_nat_spec(tm, d, prev), _const_spec((d, d)), _const_spec((1, d)),
                     _const_spec((1, d))]),
        out_specs=_nat_spec(tm, d, prev),
        out_shape=jax.ShapeDtypeStruct((bsz, seq, d), F32),
        scratch_shapes=[
            pltpu.VMEM((2, n_heads, tm, HEAD_DIM), F32),
            pltpu.VMEM((n_heads, 4, tm // 4, HEAD_DIM), F32),
            pltpu.VMEM((2, tm, LANES), F32),
            pltpu.VMEM((4, tm // 4, LANES), F32),
            pltpu.VMEM((tm, d), F32),
            xbuf, xbuf,
        ],
        compiler_params=pltpu.CompilerParams(
            dimension_semantics=("arbitrary",), vmem_limit_bytes=VMEM_LIMIT_BYTES),
        name="attn_out",
    )(*os, *ls, h3d, w_o, ln_g, ln_b)


def _cast_pad_rows_kernel(x_ref, o_ref):
    n = x_ref.shape[0]
    o_ref[0:n, :] = x_ref[...].astype(BF16)
    if o_ref.shape[0] > n:
        o_ref[n:, :] = jnp.zeros((o_ref.shape[0] - n, o_ref.shape[1]), BF16)


def _cast_chunks_kernel(x_ref, o_ref):
    nf, _, tf = o_ref.shape
    f = x_ref.shape[1]
    for c in range(nf):
        width = min(tf, f - c * tf)
        o_ref[c, :, :width] = x_ref[:, c * tf:c * tf + width].astype(BF16)
        if width < tf:
            o_ref[c, :, width:] = jnp.zeros((o_ref.shape[1], tf - width), BF16)


def _ffn_weights(w_up, w_down, tf, *, rows=256, cols=256):
    n_layers, d, two_f = w_up.shape
    f = two_f // 2
    fp = f + (-f % tf)
    nf = fp // tf
    params = pltpu.CompilerParams(dimension_semantics=("arbitrary",) * 3,
                                  vmem_limit_bytes=VMEM_LIMIT_BYTES)
    w_gv = pl.pallas_call(
        _cast_chunks_kernel,
        grid=(n_layers, d // rows, 2),
        in_specs=[pl.BlockSpec((None, rows, f), lambda l, r, j: (l, r, j))],
        out_specs=pl.BlockSpec((None, nf, None, rows, tf), lambda l, r, j: (l, 0, j, r, 0)),
        out_shape=jax.ShapeDtypeStruct((n_layers, nf, 2, d, tf), BF16),
        compiler_params=params,
        name="ffn_w_up_bf16",
    )(w_up)
    w_dn = pl.pallas_call(
        _cast_pad_rows_kernel,
        grid=(n_layers, d // cols, 1),
        in_specs=[pl.BlockSpec((None, f, cols), lambda l, c, _: (l, 0, c))],
        out_specs=pl.BlockSpec((None, fp, cols), lambda l, c, _: (l, 0, c)),
        out_shape=jax.ShapeDtypeStruct((n_layers, fp, d), BF16),
        compiler_params=params,
        name="ffn_w_down_bf16",
    )(w_down)
    return w_gv, w_dn, fp


def _conv_params(conv_w, conv_b, fp):
    two_f = conv_w.shape[1]
    f = two_f // 2
    fill = jnp.zeros((SUBLANES - CONV_WIDTH - 1, two_f), F32)
    cp = jnp.concatenate([conv_w, conv_b[None, :], fill], axis=0)
    z = jnp.zeros((SUBLANES, fp - f), F32)
    return jnp.concatenate([cp[:, :f], z, cp[:, f:], z], axis=1)


def kernel(x, pool_w_in, pool_w_grp, pool_scale, pool_w_out, attn_w_q, attn_w_o, shared_w_k,
           shared_w_v, ffn_w_up, ffn_conv_w, ffn_conv_b, ffn_w_down, ln1_g, ln1_b, ln2_g, ln2_b):
    bsz, seq, d = x.shape
    n_a = pool_w_in.shape[0]
    n_layers = ffn_w_up.shape[0]
    tf = 2 * MXU_DIM
    row = lambda a: a.reshape(1, d)

    w_gv, w_dn, fp = _ffn_weights(ffn_w_up, ffn_w_down, tf)
    h = x.reshape(bsz * seq, d)
    h_bf = None
    kv = None
    for i in range(n_layers):
        if i < n_a:
            h = _pool_layer(h, pool_w_in[i].astype(BF16), pool_w_grp[i].astype(BF16),
                            row(pool_scale[i]), pool_w_out[i].astype(BF16),
                            row(ln1_g[i]), row(ln1_b[i]), seq=seq)
        else:
            h3d = h.reshape(bsz, seq, d)
            x3d = h3d if h_bf is None else h_bf.reshape(bsz, seq, d)
            if kv is None:
                kv = (_proj(x3d, shared_w_k.astype(BF16)), _proj(x3d, shared_w_v.astype(BF16)))
            qs = _proj(x3d, attn_w_q[i - n_a].astype(BF16), out_scale=QK_SCALE_LOG2)
            outs = [_attn_branch(q, k, v) for q, k, v in zip(qs, kv[0], kv[1])]
            h = _attn_out([o for o, _ in outs], [l for _, l in outs], h3d,
                          attn_w_o[i - n_a].astype(BF16), row(ln1_g[i]), row(ln1_b[i]))
            h = h.reshape(bsz * seq, d)
        cp = _conv_params(ffn_conv_w[i], ffn_conv_b[i], fp)
        feeds_attention = n_a <= i + 1 < n_layers
        outs = _ffn_layer(h, w_gv, cp, w_dn, row(ln2_g[i]), row(ln2_b[i]), layer=i, seq=seq,
                          bf16_copy=feeds_attention)
        h, h_bf = (outs[0], outs[1]) if feeds_attention else (outs[0], None)
    return h.reshape(bsz, seq, d)
```

```python
import functools
import math

import jax
import jax.numpy as jnp
from jax import lax
from jax.experimental import pallas as pl
from jax.experimental.pallas import tpu as pltpu

POOL_WINDOWS = (2, 4, 8, 16)
HEAD_DIM = 128
DILATED_BRANCHES = ((128, 1), (512, 4), (2048, 16))
ATTN_BLOCK = 128
CONV_WIDTH = 3
DEPTH = 2
DEEPNORM_ALPHA = (2.0 * DEPTH) ** 0.25
LN_EPS = 1e-5
NEG_INF = -1e30

LANES = 128
SUBLANES = 8
MXU_DIM = 256
VMEM_LIMIT_BYTES = 56 * 1024 * 1024

DILATIONS = tuple(d for _, d in DILATED_BRANCHES)
ATTN_ROWS_PER_STEP = 8 * ATTN_BLOCK
ATTN_BLOCKS_PER_ITER = 32
QK_SCALE_LOG2 = math.log2(math.e) / math.sqrt(HEAD_DIM)
POOL_HALO = max(POOL_WINDOWS)

F32 = jnp.float32
BF16 = jnp.bfloat16


def _layer_norm(y, g, b):
    mu = jnp.mean(y, axis=-1, keepdims=True)
    d = y - mu
    var = jnp.mean(d * d, axis=-1, keepdims=True)
    return d * lax.rsqrt(var + LN_EPS) * g + b


def _const_spec(shape):
    nd = len(shape)
    return pl.BlockSpec(shape, lambda *_: (0,) * nd, pipeline_mode=pl.Buffered(1))


def _pool_kernel(x_ref, win_ref, wgrp_ref, scale_ref, wout_ref, g_ref, b_ref, o_ref,
                 pbuf, mbuf, *, tm, blocks_per_seq):
    i = pl.program_id(0)
    blk = i % blocks_per_seq
    d_model = x_ref.shape[1]
    gdim = d_model // len(POOL_WINDOWS)

    @pl.when(blk == 0)
    def _():
        pbuf[0:POOL_HALO, :] = jnp.zeros((POOL_HALO, d_model), F32)

    @pl.when(blk != 0)
    def _():
        pbuf[0:POOL_HALO, :] = pbuf[tm:tm + POOL_HALO, :]

    x = x_ref[...]
    p = jnp.dot(x.astype(BF16), win_ref[...], preferred_element_type=F32)
    pbuf[POOL_HALO:POOL_HALO + tm, :] = p

    pos = blk * tm + lax.broadcasted_iota(jnp.int32, (tm, 1), 0)
    for g, w in enumerate(POOL_WINDOWS):
        cols = slice(g * gdim, (g + 1) * gdim)
        s = pbuf[:, cols]
        shift = 1
        while shift < w:
            s = s + pltpu.roll(s, shift, axis=0)
            shift *= 2
        cnt = jnp.minimum(pos + 1, w).astype(F32)
        pooled = s[POOL_HALO:, :] / cnt - p[:, cols]
        mixed = jnp.dot(pooled.astype(BF16), wgrp_ref[g], preferred_element_type=F32)
        mbuf[:, cols] = (mixed * scale_ref[:, cols]).astype(BF16)

    mix = jnp.dot(mbuf[...], wout_ref[...], preferred_element_type=F32)
    o_ref[...] = _layer_norm(DEEPNORM_ALPHA * x + mix, g_ref[...], b_ref[...])


def _pool_layer(x2d, w_in, w_grp, scale, w_out, ln_g, ln_b, *, seq, tm=512):
    t, d = x2d.shape
    ng, gdim, _ = w_grp.shape
    kern = functools.partial(_pool_kernel, tm=tm, blocks_per_seq=seq // tm)
    return pl.pallas_call(
        kern,
        grid=(t // tm,),
        in_specs=[
            pl.BlockSpec((tm, d), lambda i: (i, 0)),
            _const_spec((d, d)),
            _const_spec((ng, gdim, gdim)),
            _const_spec((1, d)),
            _const_spec((d, d)),
            _const_spec((1, d)),
            _const_spec((1, d)),
        ],
        out_specs=pl.BlockSpec((tm, d), lambda i: (i, 0)),
        out_shape=jax.ShapeDtypeStruct((t, d), F32),
        scratch_shapes=[
            pltpu.VMEM((POOL_HALO + tm, d), F32),
            pltpu.VMEM((tm, d), BF16),
        ],
        compiler_params=pltpu.CompilerParams(
            dimension_semantics=("arbitrary",), vmem_limit_bytes=VMEM_LIMIT_BYTES),
        name="pool_layer",
    )(x2d, w_in, w_grp, scale, w_out, ln_g, ln_b)


def _ffn_kernel(h_ref, wgv_ref, cp_ref, wd_ref, g_ref, b_ref, *refs,
                tm, nf, blocks_per_seq, n_out):
    out_refs = refs[:n_out]
    hb, acc, ug_buf, uv_buf, carry_g, carry_v, *act_bufs = refs[n_out:]
    i = pl.program_id(0)
    f = pl.program_id(1)
    seq_start = (i % blocks_per_seq) == 0
    wg_ref, wv_ref = wgv_ref.at[0], wgv_ref.at[1]
    tf = wgv_ref.shape[2]
    fp = nf * tf

    def conv(uc, buf, carry, cp_base, c):
        cs = slice(c * LANES, (c + 1) * LANES)
        cp = cp_ref[:, pl.ds(pl.multiple_of(cp_base + f * tf + c * LANES, LANES), LANES)]
        buf[c, 0:SUBLANES, :] = jnp.where(seq_start, 0.0, carry[f, :, cs])
        buf[c, SUBLANES:SUBLANES + tm, :] = uc
        carry[f, :, cs] = uc[tm - SUBLANES:tm, :]
        out = cp[CONV_WIDTH:CONV_WIDTH + 1, :] + uc * cp[CONV_WIDTH - 1:CONV_WIDTH, :]
        for lag in range(1, CONV_WIDTH):
            tap = CONV_WIDTH - 1 - lag
            out = out + buf[c, SUBLANES - lag:SUBLANES - lag + tm, :] * cp[tap:tap + 1, :]
        return out

    def up_stage(act_dst):
        ug = jnp.dot(hb[...], wg_ref[...], preferred_element_type=F32)
        uv = jnp.dot(hb[...], wv_ref[...], preferred_element_type=F32)
        for c in range(tf // LANES):
            cs = slice(c * LANES, (c + 1) * LANES)
            gate = conv(ug[:, cs], ug_buf, carry_g, 0, c)
            val = conv(uv[:, cs], uv_buf, carry_v, fp, c)
            sig_den = 1.0 + jnp.exp2(gate * (-math.log2(math.e)))
            act_dst[:, cs] = (gate / sig_den * val).astype(BF16)

    def down_stage(act_src):
        acc[...] += jnp.dot(act_src[...], wd_ref[...], preferred_element_type=F32)

    @pl.when(f == 0)
    def _():
        hb[...] = h_ref[...].astype(BF16)
        acc[...] = jnp.zeros(acc.shape, F32)
        up_stage(act_bufs[0])

    for parity in range(2):
        @pl.when((f > 0) & (f < nf) & (f % 2 == parity))
        def _():
            up_stage(act_bufs[parity])
            down_stage(act_bufs[1 - parity])

    @pl.when(f == nf)
    def _():
        act_src = act_bufs[(nf - 1) % 2]
        half = tm // 2
        for rows in (slice(0, half), slice(half, tm)):
            mix = acc[rows, :] + jnp.dot(act_src[rows, :], wd_ref[...],
                                         preferred_element_type=F32)
            out = _layer_norm(DEEPNORM_ALPHA * h_ref[rows, :] + mix, g_ref[...], b_ref[...])
            for o_ref in out_refs:
                o_ref[rows, :] = out.astype(o_ref.dtype)


def _ffn_layer(h2d, w_gv, cp, w_down, ln_g, ln_b, *, layer, seq, bf16_copy, tm=512):
    t, d = h2d.shape
    nf, tf = w_gv.shape[1], w_gv.shape[4]
    fp = nf * tf
    out_dtypes = (F32, BF16) if bf16_copy else (F32,)
    kern = functools.partial(_ffn_kernel, tm=tm, nf=nf, blocks_per_seq=seq // tm,
                             n_out=len(out_dtypes))
    return pl.pallas_call(
        kern,
        grid=(t // tm, nf + 1),
        in_specs=[
            pl.BlockSpec((tm, d), lambda i, f: (i, 0)),
            pl.BlockSpec((None, None, 2, d, tf),
                         lambda i, f: (layer, jnp.minimum(f, nf - 1), 0, 0, 0)),
            _const_spec((SUBLANES, 2 * fp)),
            pl.BlockSpec((None, tf, d), lambda i, f: (layer, jnp.maximum(f - 1, 0), 0)),
            _const_spec((1, d)),
            _const_spec((1, d)),
        ],
        out_specs=[pl.BlockSpec((tm, d), lambda i, f: (i, 0)) for _ in out_dtypes],
        out_shape=[jax.ShapeDtypeStruct((t, d), dt) for dt in out_dtypes],
        scratch_shapes=[
            pltpu.VMEM((tm, d), BF16),
            pltpu.VMEM((tm, d), F32),
            pltpu.VMEM((tf // LANES, SUBLANES + tm, LANES), F32),
            pltpu.VMEM((tf // LANES, SUBLANES + tm, LANES), F32),
            pltpu.VMEM((nf, SUBLANES, tf), F32),
            pltpu.VMEM((nf, SUBLANES, tf), F32),
            pltpu.VMEM((tm, tf), BF16),
            pltpu.VMEM((tm, tf), BF16),
        ],
        compiler_params=pltpu.CompilerParams(
            dimension_semantics=("arbitrary", "arbitrary"), vmem_limit_bytes=VMEM_LIMIT_BYTES),
        name="ffn_layer",
    )(h2d, w_gv, cp, w_down, ln_g, ln_b)


def _lagged_block(nblocks, blocks_per_seq, lag):
    def split(g):
        j = jnp.clip(g - lag, 0, nblocks - 1)
        return j // blocks_per_seq, j % blocks_per_seq
    return split


def _nat_spec(tm, width, split):
    def idx(g):
        b, i = split(g)
        return (b, i, 0)
    return pl.BlockSpec((None, tm, width), idx)


def _dilated_spec(dil, tm, width, split):
    def idx(g):
        b, i = split(g)
        return (b, 0, i, 0)
    return pl.BlockSpec((None, dil, tm // dil, width), idx)


def _proj_kernel(x_ref, w_ref, *refs, nblocks, out_scale):
    nat_ref, o4_ref, o16_ref = refs[:3]
    ybufs, zbuf = refs[3:5], refs[5]
    n_tiles, tm, _ = ybufs[0].shape
    g = pl.program_id(0)
    tiles_per_dot = MXU_DIM // LANES

    def matmul(ybuf, xb, n):
        cols = slice(n * MXU_DIM, (n + 1) * MXU_DIM)
        y = jnp.dot(xb, w_ref[:, cols], preferred_element_type=F32)
        if out_scale != 1.0:
            y = y * out_scale
        nat_ref[0, :, cols] = y.astype(BF16)
        for t in range(tiles_per_dot):
            ybuf[n * tiles_per_dot + t] = y[:, t * LANES:(t + 1) * LANES]

    def scatter(ybuf, n):
        for c in range(n * tiles_per_dot, (n + 1) * tiles_per_dot):
            cs = slice(c * LANES, (c + 1) * LANES)
            for r in range(4):
                slab = ybuf[c, pl.ds(r, tm // 4, stride=4), :]
                o4_ref[r, :, cs] = slab.astype(BF16)
                zbuf[c, r] = slab
            for r in range(4):
                for q in range(4):
                    o16_ref[4 * q + r, :, cs] = (
                        zbuf[c, r, pl.ds(q, tm // 16, stride=4), :].astype(BF16))

    def step(y_dst, y_src):
        xb = None if y_dst is None else x_ref[...].astype(BF16)
        for n in range(n_tiles // tiles_per_dot):
            if y_dst is not None:
                matmul(y_dst, xb, n)
            if y_src is not None:
                scatter(y_src, n)

    @pl.when(g == 0)
    def _():
        step(ybufs[0], None)

    for parity in range(2):
        @pl.when((g > 0) & (g < nblocks) & (g % 2 == parity))
        def _():
            step(ybufs[parity], ybufs[1 - parity])

    @pl.when(g == nblocks)
    def _():
        step(None, ybufs[(nblocks - 1) % 2])


def _proj(h3d, w, *, out_scale=1.0, tm=512):
    bsz, seq, d = h3d.shape
    blocks_per_seq = seq // tm
    nblocks = bsz * blocks_per_seq
    cur = _lagged_block(nblocks, blocks_per_seq, 0)
    prev = _lagged_block(nblocks, blocks_per_seq, 1)
    assert DILATIONS == (1, 4, 16), "the two-step row de-interleave is written for 1, 4, 16"
    ybuf = pltpu.VMEM((d // LANES, tm, LANES), F32)
    zbuf = pltpu.VMEM((d // LANES, 4, tm // 4, LANES), F32)
    return pl.pallas_call(
        functools.partial(_proj_kernel, nblocks=nblocks, out_scale=out_scale),
        grid=(nblocks + 1,),
        in_specs=[_nat_spec(tm, d, cur), _const_spec((d, d))],
        out_specs=[_dilated_spec(1, tm, d, cur), _dilated_spec(4, tm, d, prev),
                   _dilated_spec(16, tm, d, prev)],
        out_shape=[jax.ShapeDtypeStruct((bsz, dil, seq // dil, d), BF16) for dil in DILATIONS],
        scratch_shapes=[ybuf, ybuf, zbuf],
        compiler_params=pltpu.CompilerParams(
            dimension_semantics=("arbitrary",), vmem_limit_bytes=VMEM_LIMIT_BYTES),
        name="proj",
    )(h3d, w)


def _attn_kernel(q_ref, *refs, qb, n_heads, heads_per_iter, halo):
    if halo:
        kp_ref, kc_ref, vp_ref, vc_ref, o_ref, lse_ref, bias = refs
    else:
        kc_ref, vc_ref, o_ref, lse_ref, bias = refs
        kp_ref, vp_ref = kc_ref.at[0:ATTN_BLOCK], vc_ref.at[0:ATTN_BLOCK]
    i = pl.program_id(2)
    nblk = qb // ATTN_BLOCK
    blk = ATTN_BLOCK

    row = lax.broadcasted_iota(jnp.int32, (blk, 2 * blk), 0)
    col = lax.broadcasted_iota(jnp.int32, (blk, 2 * blk), 1)
    dist = blk + row - col
    band = (dist >= 0) & (dist <= blk)
    has_prev = jnp.broadcast_to(i > 0, band.shape)
    bias[1] = jnp.where(band, 0.0, NEG_INF)
    bias[0] = jnp.where(band & ((col >= blk) | has_prev), 0.0, NEG_INF)

    lane = lax.broadcasted_iota(jnp.int32, (blk, LANES), 1)
    ones = jnp.ones((2 * blk, HEAD_DIM), BF16)
    lse_ref[...] = jnp.zeros(lse_ref.shape, F32)

    def one_block(h, j):
        hc = pl.ds(pl.multiple_of(h * HEAD_DIM, HEAD_DIM), HEAD_DIM)
        rows = slice(j * blk, (j + 1) * blk)
        q = q_ref[rows, hc]
        if j == 0:
            kk = jnp.concatenate([kp_ref[:, hc], kc_ref[0:blk, hc]], axis=0)
            vv = jnp.concatenate([vp_ref[:, hc], vc_ref[0:blk, hc]], axis=0)
        else:
            kk = kc_ref[(j - 1) * blk:(j + 1) * blk, hc]
            vv = vc_ref[(j - 1) * blk:(j + 1) * blk, hc]
        s = lax.dot_general(q, kk, (((1,), (1,)), ((), ())), preferred_element_type=F32)
        s = s + bias[min(j, 1)]
        m = jnp.max(s, axis=-1, keepdims=True)
        p = jnp.exp2(s - m)
        ov = jnp.dot(p.astype(BF16), jnp.concatenate([vv, ones], axis=1),
                     preferred_element_type=F32)
        den = ov[:, HEAD_DIM:]
        o_ref[rows, hc] = (ov[:, :HEAD_DIM] / den).astype(o_ref.dtype)
        lse = m + jnp.log2(den)
        lse_ref[rows, :] = jnp.where(lane == h, lse, lse_ref[rows, :])

    def head_group(g, carry):
        for hh in range(heads_per_iter):
            for j in range(nblk):
                one_block(g * heads_per_iter + hh, j)
        return carry

    lax.fori_loop(0, n_heads // heads_per_iter, head_group, 0)


def _attn_branch(q, k, v):
    bsz, dil, length, d = q.shape
    n_heads = d // HEAD_DIM
    qb = min(length, ATTN_ROWS_PER_STEP)
    halo = length > qb
    per = qb // ATTN_BLOCK
    heads_per_iter = max(1, ATTN_BLOCKS_PER_ITER // per)
    cur = lambda width: pl.BlockSpec((None, None, qb, width), lambda b, r, i: (b, r, i, 0))
    prev = pl.BlockSpec((None, None, ATTN_BLOCK, d),
                        lambda b, r, i: (b, r, jnp.maximum(i * per - 1, 0), 0))
    kern = functools.partial(_attn_kernel, qb=qb, n_heads=n_heads,
                             heads_per_iter=heads_per_iter, halo=halo)
    return pl.pallas_call(
        kern,
        grid=(bsz, dil, length // qb),
        in_specs=[cur(d), prev, cur(d), prev, cur(d)] if halo else [cur(d)] * 3,
        out_specs=[cur(d), cur(LANES)],
        out_shape=[
            jax.ShapeDtypeStruct((bsz, dil, length, d), BF16),
            jax.ShapeDtypeStruct((bsz, dil, length, LANES), F32),
        ],
        scratch_shapes=[pltpu.VMEM((2, ATTN_BLOCK, 2 * ATTN_BLOCK), F32)],
        compiler_params=pltpu.CompilerParams(
            dimension_semantics=("arbitrary", "arbitrary", "arbitrary"),
            vmem_limit_bytes=VMEM_LIMIT_BYTES),
        name=f"attn_d{dil}",
    )(*((q, k, k, v, v) if halo else (q, k, v)))


def _attn_out_kernel(o1_ref, o4_ref, o16_ref, l1_ref, l4_ref, l16_ref, h_ref, wo_ref, g_ref,
                     b_ref, out_ref, o_nat, z16, l_nat, zl16, ybuf, *xbufs, n_heads, nblocks):
    tm = h_ref.shape[0]
    g = pl.program_id(0)
    heads_per_dot = MXU_DIM // HEAD_DIM

    def to_natural(dst, src4):
        for r in range(4):
            dst[pl.ds(r, tm // 4, stride=4), :] = src4(r)

    def from_d16(tmp, src16):
        for r in range(4):
            for q in range(4):
                tmp[r, pl.ds(q, tm // 16, stride=4), :] = src16(4 * q + r)
        return lambda r: tmp[r]

    def merge_weights():
        to_natural(l_nat.at[0], lambda r: l4_ref[r])
        to_natural(l_nat.at[1], from_d16(zl16, lambda r: l16_ref[r]))
        lses = [l1_ref[0], l_nat[0], l_nat[1]]
        mx = functools.reduce(jnp.maximum, lses)
        es = [jnp.exp2(l - mx) for l in lses]
        tot = functools.reduce(lambda a, b: a + b, es)
        return es[0] / tot, es[1] / tot

    def merge_head(xbuf, w1, w4, h):
        hc = slice(h * HEAD_DIM, (h + 1) * HEAD_DIM)
        to_natural(o_nat.at[0, h], lambda r: o4_ref[r, :, hc].astype(F32))
        to_natural(o_nat.at[1, h],
                   from_d16(z16.at[h], lambda r: o16_ref[r, :, hc].astype(F32)))
        o1 = o1_ref[0, :, hc].astype(F32)
        o4 = o_nat[0, h]
        o16 = o_nat[1, h]
        shape = (tm, HEAD_DIM)
        o = (o16 + jnp.broadcast_to(w1[:, h:h + 1], shape) * (o1 - o16)
             + jnp.broadcast_to(w4[:, h:h + 1], shape) * (o4 - o16))
        xbuf[:, hc] = o.astype(BF16)

    def step(x_dst, x_src):
        if x_dst is not None:
            w1, w4 = merge_weights()
        for n in range(n_heads // heads_per_dot):
            if x_src is not None:
                cols = slice(n * MXU_DIM, (n + 1) * MXU_DIM)
                ybuf[:, cols] = jnp.dot(x_src[...], wo_ref[:, cols], preferred_element_type=F32)
            if x_dst is not None:
                for h in range(n * heads_per_dot, (n + 1) * heads_per_dot):
                    merge_head(x_dst, w1, w4, h)
        if x_src is not None:
            y = DEEPNORM_ALPHA * h_ref[...] + ybuf[...]
            out_ref[...] = _layer_norm(y, g_ref[...], b_ref[...])

    @pl.when(g == 0)
    def _():
        step(xbufs[0], None)

    for parity in range(2):
        @pl.when((g > 0) & (g < nblocks) & (g % 2 == parity))
        def _():
            step(xbufs[parity], xbufs[1 - parity])

    @pl.when(g == nblocks)
    def _():
        step(None, xbufs[(nblocks - 1) % 2])


def _attn_out(os, ls, h3d, w_o, ln_g, ln_b, *, tm=256):
    bsz, seq, d = h3d.shape
    assert DILATIONS == (1, 4, 16), "the two-step row interleave is written for 1, 4, 16"
    n_heads = d // HEAD_DIM
    blocks_per_seq = seq // tm
    nblocks = bsz * blocks_per_seq
    cur = _lagged_block(nblocks, blocks_per_seq, 0)
    prev = _lagged_block(nblocks, blocks_per_seq, 1)
    xbuf = pltpu.VMEM((tm, d), BF16)
    kern = functools.partial(_attn_out_kernel, n_heads=n_heads, nblocks=nblocks)
    return pl.pallas_call(
        kern,
        grid=(nblocks + 1,),
        in_specs=([_dilated_spec(dil, tm, d, cur) for dil in DILATIONS]
                  + [_dilated_spec(dil, tm, LANES, cur) for dil in DILATIONS]
                  + [_nat_spec(tm, d, prev), _const_spec((d, d)), _const_spec((1, d)),
                     _const_spec((1, d))]),
        out_specs=_nat_spec(tm, d, prev),
        out_shape=jax.ShapeDtypeStruct((bsz, seq, d), F32),
        scratch_shapes=[
            pltpu.VMEM((2, n_heads, tm, HEAD_DIM), F32),
            pltpu.VMEM((n_heads, 4, tm // 4, HEAD_DIM), F32),
            pltpu.VMEM((2, tm, LANES), F32),
            pltpu.VMEM((4, tm // 4, LANES), F32),
            pltpu.VMEM((tm, d), F32),
            xbuf, xbuf,
        ],
        compiler_params=pltpu.CompilerParams(
            dimension_semantics=("arbitrary",), vmem_limit_bytes=VMEM_LIMIT_BYTES),
        name="attn_out",
    )(*os, *ls, h3d, w_o, ln_g, ln_b)


def _cast_pad_rows_kernel(x_ref, o_ref):
    n = x_ref.shape[0]
    o_ref[0:n, :] = x_ref[...].astype(BF16)
    if o_ref.shape[0] > n:
        o_ref[n:, :] = jnp.zeros((o_ref.shape[0] - n, o_ref.shape[1]), BF16)


def _cast_chunks_kernel(x_ref, o_ref):
    nf, _, tf = o_ref.shape
    f = x_ref.shape[1]
    for c in range(nf):
        width = min(tf, f - c * tf)
        o_ref[c, :, :width] = x_ref[:, c * tf:c * tf + width].astype(BF16)
        if width < tf:
            o_ref[c, :, width:] = jnp.zeros((o_ref.shape[1], tf - width), BF16)


def _ffn_weights(w_up, w_down, tf, *, rows=256, cols=256):
    n_layers, d, two_f = w_up.shape
    f = two_f // 2
    fp = f + (-f % tf)
    nf = fp // tf
    params = pltpu.CompilerParams(dimension_semantics=("arbitrary",) * 3,
                                  vmem_limit_bytes=VMEM_LIMIT_BYTES)
    w_gv = pl.pallas_call(
        _cast_chunks_kernel,
        grid=(n_layers, d // rows, 2),
        in_specs=[pl.BlockSpec((None, rows, f), lambda l, r, j: (l, r, j))],
        out_specs=pl.BlockSpec((None, nf, None, rows, tf), lambda l, r, j: (l, 0, j, r, 0)),
        out_shape=jax.ShapeDtypeStruct((n_layers, nf, 2, d, tf), BF16),
        compiler_params=params,
        name="ffn_w_up_bf16",
    )(w_up)
    w_dn = pl.pallas_call(
        _cast_pad_rows_kernel,
        grid=(n_layers, d // cols, 1),
        in_specs=[pl.BlockSpec((None, f, cols), lambda l, c, _: (l, 0, c))],
        out_specs=pl.BlockSpec((None, fp, cols), lambda l, c, _: (l, 0, c)),
        out_shape=jax.ShapeDtypeStruct((n_layers, fp, d), BF16),
        compiler_params=params,
        name="ffn_w_down_bf16",
    )(w_down)
    return w_gv, w_dn, fp


def _conv_params(conv_w, conv_b, fp):
    two_f = conv_w.shape[1]
    f = two_f // 2
    fill = jnp.zeros((SUBLANES - CONV_WIDTH - 1, two_f), F32)
    cp = jnp.concatenate([conv_w, conv_b[None, :], fill], axis=0)
    z = jnp.zeros((SUBLANES, fp - f), F32)
    return jnp.concatenate([cp[:, :f], z, cp[:, f:], z], axis=1)


def kernel(x, pool_w_in, pool_w_grp, pool_scale, pool_w_out, attn_w_q, attn_w_o, shared_w_k,
           shared_w_v, ffn_w_up, ffn_conv_w, ffn_conv_b, ffn_w_down, ln1_g, ln1_b, ln2_g, ln2_b):
    bsz, seq, d = x.shape
    n_a = pool_w_in.shape[0]
    n_layers = ffn_w_up.shape[0]
    tf = 2 * MXU_DIM
    row = lambda a: a.reshape(1, d)

    w_gv, w_dn, fp = _ffn_weights(ffn_w_up, ffn_w_down, tf)
    h = x.reshape(bsz * seq, d)
    h_bf = None
    kv = None
    for i in range(n_layers):
        if i < n_a:
            h = _pool_layer(h, pool_w_in[i].astype(BF16), pool_w_grp[i].astype(BF16),
                            row(pool_scale[i]), pool_w_out[i].astype(BF16),
                            row(ln1_g[i]), row(ln1_b[i]), seq=seq)
        else:
            h3d = h.reshape(bsz, seq, d)
            x3d = h3d if h_bf is None else h_bf.reshape(bsz, seq, d)
            if kv is None:
                kv = (_proj(x3d, shared_w_k.astype(BF16)), _proj(x3d, shared_w_v.astype(BF16)))
            qs = _proj(x3d, attn_w_q[i - n_a].astype(BF16), out_scale=QK_SCALE_LOG2)
            outs = [_attn_branch(q, k, v) for q, k, v in zip(qs, kv[0], kv[1])]
            h = _attn_out([o for o, _ in outs], [l for _, l in outs], h3d,
                          attn_w_o[i - n_a].astype(BF16), row(ln1_g[i]), row(ln1_b[i]))
            h = h.reshape(bsz * seq, d)
        cp = _conv_params(ffn_conv_w[i], ffn_conv_b[i], fp)
        feeds_attention = n_a <= i + 1 < n_layers
        outs = _ffn_layer(h, w_gv, cp, w_dn, row(ln2_g[i]), row(ln2_b[i]), layer=i, seq=seq,
                          bf16_copy=feeds_attention)
        h, h_bf = (outs[0], outs[1]) if feeds_attention else (outs[0], None)
    return h.reshape(bsz, seq, d)
```

```python
import functools
import math

import jax
import jax.numpy as jnp
from jax import lax
from jax.experimental import pallas as pl
from jax.experimental.pallas import tpu as pltpu

POOL_WINDOWS = (2, 4, 8, 16)
HEAD_DIM = 128
DILATED_BRANCHES = ((128, 1), (512, 4), (2048, 16))
ATTN_BLOCK = 128
CONV_WIDTH = 3
DEPTH = 2
DEEPNORM_ALPHA = (2.0 * DEPTH) ** 0.25
LN_EPS = 1e-5
NEG_INF = -1e30

LANES = 128
SUBLANES = 8
MXU_DIM = 256
VMEM_LIMIT_BYTES = 56 * 1024 * 1024

DILATIONS = tuple(d for _, d in DILATED_BRANCHES)
ATTN_ROWS_PER_STEP = 8 * ATTN_BLOCK
ATTN_BLOCKS_PER_ITER = 64
QK_SCALE_LOG2 = math.log2(math.e) / math.sqrt(HEAD_DIM)
POOL_HALO = max(POOL_WINDOWS)

F32 = jnp.float32
BF16 = jnp.bfloat16


def _layer_norm(y, g, b):
    mu = jnp.mean(y, axis=-1, keepdims=True)
    d = y - mu
    var = jnp.mean(d * d, axis=-1, keepdims=True)
    return d * lax.rsqrt(var + LN_EPS) * g + b


def _const_spec(shape):
    nd = len(shape)
    return pl.BlockSpec(shape, lambda *_: (0,) * nd, pipeline_mode=pl.Buffered(1))


def _pool_kernel(x_ref, win_ref, wgrp_ref, scale_ref, wout_ref, g_ref, b_ref, o_ref,
                 pbuf, mbuf, *, tm, blocks_per_seq):
    i = pl.program_id(0)
    blk = i % blocks_per_seq
    d_model = x_ref.shape[1]
    gdim = d_model // len(POOL_WINDOWS)

    @pl.when(blk == 0)
    def _():
        pbuf[0:POOL_HALO, :] = jnp.zeros((POOL_HALO, d_model), F32)

    @pl.when(blk != 0)
    def _():
        pbuf[0:POOL_HALO, :] = pbuf[tm:tm + POOL_HALO, :]

    x = x_ref[...]
    p = jnp.dot(x.astype(BF16), win_ref[...], preferred_element_type=F32)
    pbuf[POOL_HALO:POOL_HALO + tm, :] = p

    pos = blk * tm + lax.broadcasted_iota(jnp.int32, (tm, 1), 0)
    for g, w in enumerate(POOL_WINDOWS):
        cols = slice(g * gdim, (g + 1) * gdim)
        s = pbuf[:, cols]
        shift = 1
        while shift < w:
            s = s + pltpu.roll(s, shift, axis=0)
            shift *= 2
        cnt = jnp.minimum(pos + 1, w).astype(F32)
        pooled = s[POOL_HALO:, :] / cnt - p[:, cols]
        mixed = jnp.dot(pooled.astype(BF16), wgrp_ref[g], preferred_element_type=F32)
        mbuf[:, cols] = (mixed * scale_ref[:, cols]).astype(BF16)

    mix = jnp.dot(mbuf[...], wout_ref[...], preferred_element_type=F32)
    o_ref[...] = _layer_norm(DEEPNORM_ALPHA * x + mix, g_ref[...], b_ref[...])


def _pool_layer(x2d, w_in, w_grp, scale, w_out, ln_g, ln_b, *, seq, tm=512):
    t, d = x2d.shape
    ng, gdim, _ = w_grp.shape
    kern = functools.partial(_pool_kernel, tm=tm, blocks_per_seq=seq // tm)
    return pl.pallas_call(
        kern,
        grid=(t // tm,),
        in_specs=[
            pl.BlockSpec((tm, d), lambda i: (i, 0)),
            _const_spec((d, d)),
            _const_spec((ng, gdim, gdim)),
            _const_spec((1, d)),
            _const_spec((d, d)),
            _const_spec((1, d)),
            _const_spec((1, d)),
        ],
        out_specs=pl.BlockSpec((tm, d), lambda i: (i, 0)),
        out_shape=jax.ShapeDtypeStruct((t, d), F32),
        scratch_shapes=[
            pltpu.VMEM((POOL_HALO + tm, d), F32),
            pltpu.VMEM((tm, d), BF16),
        ],
        compiler_params=pltpu.CompilerParams(
            dimension_semantics=("arbitrary",), vmem_limit_bytes=VMEM_LIMIT_BYTES),
        name="pool_layer",
    )(x2d, w_in, w_grp, scale, w_out, ln_g, ln_b)


def _ffn_kernel(h_ref, wgv_ref, cp_ref, wd_ref, g_ref, b_ref, *refs,
                tm, nf, blocks_per_seq, n_out):
    out_refs = refs[:n_out]
    hb, acc, ug_buf, uv_buf, carry_g, carry_v, *act_bufs = refs[n_out:]
    i = pl.program_id(0)
    f = pl.program_id(1)
    seq_start = (i % blocks_per_seq) == 0
    wg_ref, wv_ref = wgv_ref.at[0], wgv_ref.at[1]
    tf = wgv_ref.shape[2]
    fp = nf * tf

    def conv(uc, buf, carry, cp_base, c):
        cs = slice(c * LANES, (c + 1) * LANES)
        cp = cp_ref[:, pl.ds(pl.multiple_of(cp_base + f * tf + c * LANES, LANES), LANES)]
        buf[c, 0:SUBLANES, :] = jnp.where(seq_start, 0.0, carry[f, :, cs])
        buf[c, SUBLANES:SUBLANES + tm, :] = uc
        carry[f, :, cs] = uc[tm - SUBLANES:tm, :]
        out = cp[CONV_WIDTH:CONV_WIDTH + 1, :] + uc * cp[CONV_WIDTH - 1:CONV_WIDTH, :]
        for lag in range(1, CONV_WIDTH):
            tap = CONV_WIDTH - 1 - lag
            out = out + buf[c, SUBLANES - lag:SUBLANES - lag + tm, :] * cp[tap:tap + 1, :]
        return out

    def up_stage(act_dst):
        ug = jnp.dot(hb[...], wg_ref[...], preferred_element_type=F32)
        uv = jnp.dot(hb[...], wv_ref[...], preferred_element_type=F32)
        for c in range(tf // LANES):
            cs = slice(c * LANES, (c + 1) * LANES)
            gate = conv(ug[:, cs], ug_buf, carry_g, 0, c)
            val = conv(uv[:, cs], uv_buf, carry_v, fp, c)
            sig_den = 1.0 + jnp.exp2(gate * (-math.log2(math.e)))
            act_dst[:, cs] = (gate / sig_den * val).astype(BF16)

    def down_stage(act_src):
        acc[...] += jnp.dot(act_src[...], wd_ref[...], preferred_element_type=F32)

    @pl.when(f == 0)
    def _():
        hb[...] = h_ref[...].astype(BF16)
        acc[...] = jnp.zeros(acc.shape, F32)
        up_stage(act_bufs[0])

    for parity in range(2):
        @pl.when((f > 0) & (f < nf) & (f % 2 == parity))
        def _():
            up_stage(act_bufs[parity])
            down_stage(act_bufs[1 - parity])

    @pl.when(f == nf)
    def _():
        act_src = act_bufs[(nf - 1) % 2]
        half = tm // 2
        for rows in (slice(0, half), slice(half, tm)):
            mix = acc[rows, :] + jnp.dot(act_src[rows, :], wd_ref[...],
                                         preferred_element_type=F32)
            out = _layer_norm(DEEPNORM_ALPHA * h_ref[rows, :] + mix, g_ref[...], b_ref[...])
            for o_ref in out_refs:
                o_ref[rows, :] = out.astype(o_ref.dtype)


def _ffn_layer(h2d, w_gv, cp, w_down, ln_g, ln_b, *, layer, seq, bf16_copy, tm=512):
    t, d = h2d.shape
    nf, tf = w_gv.shape[1], w_gv.shape[4]
    fp = nf * tf
    out_dtypes = (F32, BF16) if bf16_copy else (F32,)
    kern = functools.partial(_ffn_kernel, tm=tm, nf=nf, blocks_per_seq=seq // tm,
                             n_out=len(out_dtypes))
    return pl.pallas_call(
        kern,
        grid=(t // tm, nf + 1),
        in_specs=[
            pl.BlockSpec((tm, d), lambda i, f: (i, 0)),
            pl.BlockSpec((None, None, 2, d, tf),
                         lambda i, f: (layer, jnp.minimum(f, nf - 1), 0, 0, 0)),
            _const_spec((SUBLANES, 2 * fp)),
            pl.BlockSpec((None, tf, d), lambda i, f: (layer, jnp.maximum(f - 1, 0), 0)),
            _const_spec((1, d)),
            _const_spec((1, d)),
        ],
        out_specs=[pl.BlockSpec((tm, d), lambda i, f: (i, 0)) for _ in out_dtypes],
        out_shape=[jax.ShapeDtypeStruct((t, d), dt) for dt in out_dtypes],
        scratch_shapes=[
            pltpu.VMEM((tm, d), BF16),
            pltpu.VMEM((tm, d), F32),
            pltpu.VMEM((tf // LANES, SUBLANES + tm, LANES), F32),
            pltpu.VMEM((tf // LANES, SUBLANES + tm, LANES), F32),
            pltpu.VMEM((nf, SUBLANES, tf), F32),
            pltpu.VMEM((nf, SUBLANES, tf), F32),
            pltpu.VMEM((tm, tf), BF16),
            pltpu.VMEM((tm, tf), BF16),
        ],
        compiler_params=pltpu.CompilerParams(
            dimension_semantics=("arbitrary", "arbitrary"), vmem_limit_bytes=VMEM_LIMIT_BYTES),
        name="ffn_layer",
    )(h2d, w_gv, cp, w_down, ln_g, ln_b)


def _lagged_block(nblocks, blocks_per_seq, lag):
    def split(g):
        j = jnp.clip(g - lag, 0, nblocks - 1)
        return j // blocks_per_seq, j % blocks_per_seq
    return split


def _nat_spec(tm, width, split):
    def idx(g):
        b, i = split(g)
        return (b, i, 0)
    return pl.BlockSpec((None, tm, width), idx)


def _dilated_spec(dil, tm, width, split):
    def idx(g):
        b, i = split(g)
        return (b, 0, i, 0)
    return pl.BlockSpec((None, dil, tm // dil, width), idx)


def _proj_kernel(x_ref, w_ref, *refs, nblocks, out_scale):
    nat_ref, o4_ref, o16_ref = refs[:3]
    ybufs, zbuf = refs[3:5], refs[5]
    n_tiles, tm, _ = ybufs[0].shape
    g = pl.program_id(0)
    tiles_per_dot = MXU_DIM // LANES

    def matmul(ybuf, xb, n):
        cols = slice(n * MXU_DIM, (n + 1) * MXU_DIM)
        y = jnp.dot(xb, w_ref[:, cols], preferred_element_type=F32)
        if out_scale != 1.0:
            y = y * out_scale
        nat_ref[0, :, cols] = y.astype(BF16)
        for t in range(tiles_per_dot):
            ybuf[n * tiles_per_dot + t] = y[:, t * LANES:(t + 1) * LANES]

    def scatter(ybuf, n):
        for c in range(n * tiles_per_dot, (n + 1) * tiles_per_dot):
            cs = slice(c * LANES, (c + 1) * LANES)
            for r in range(4):
                slab = ybuf[c, pl.ds(r, tm // 4, stride=4), :]
                o4_ref[r, :, cs] = slab.astype(BF16)
                zbuf[c, r] = slab
            for r in range(4):
                for q in range(4):
                    o16_ref[4 * q + r, :, cs] = (
                        zbuf[c, r, pl.ds(q, tm // 16, stride=4), :].astype(BF16))

    def step(y_dst, y_src):
        xb = None if y_dst is None else x_ref[...].astype(BF16)
        for n in range(n_tiles // tiles_per_dot):
            if y_dst is not None:
                matmul(y_dst, xb, n)
            if y_src is not None:
                scatter(y_src, n)

    @pl.when(g == 0)
    def _():
        step(ybufs[0], None)

    for parity in range(2):
        @pl.when((g > 0) & (g < nblocks) & (g % 2 == parity))
        def _():
            step(ybufs[parity], ybufs[1 - parity])

    @pl.when(g == nblocks)
    def _():
        step(None, ybufs[(nblocks - 1) % 2])


def _proj(h3d, w, *, out_scale=1.0, tm=512):
    bsz, seq, d = h3d.shape
    blocks_per_seq = seq // tm
    nblocks = bsz * blocks_per_seq
    cur = _lagged_block(nblocks, blocks_per_seq, 0)
    prev = _lagged_block(nblocks, blocks_per_seq, 1)
    assert DILATIONS == (1, 4, 16), "the two-step row de-interleave is written for 1, 4, 16"
    ybuf = pltpu.VMEM((d // LANES, tm, LANES), F32)
    zbuf = pltpu.VMEM((d // LANES, 4, tm // 4, LANES), F32)
    return pl.pallas_call(
        functools.partial(_proj_kernel, nblocks=nblocks, out_scale=out_scale),
        grid=(nblocks + 1,),
        in_specs=[_nat_spec(tm, d, cur), _const_spec((d, d))],
        out_specs=[_dilated_spec(1, tm, d, cur), _dilated_spec(4, tm, d, prev),
                   _dilated_spec(16, tm, d, prev)],
        out_shape=[jax.ShapeDtypeStruct((bsz, dil, seq // dil, d), BF16) for dil in DILATIONS],
        scratch_shapes=[ybuf, ybuf, zbuf],
        compiler_params=pltpu.CompilerParams(
            dimension_semantics=("arbitrary",), vmem_limit_bytes=VMEM_LIMIT_BYTES),
        name="proj",
    )(h3d, w)


def _attn_kernel(q_ref, *refs, qb, n_heads, heads_per_iter, halo):
    if halo:
        kp_ref, kc_ref, vp_ref, vc_ref, o_ref, lse_ref, bias = refs
    else:
        kc_ref, vc_ref, o_ref, lse_ref, bias = refs
        kp_ref, vp_ref = kc_ref.at[0:ATTN_BLOCK], vc_ref.at[0:ATTN_BLOCK]
    i = pl.program_id(2)
    nblk = qb // ATTN_BLOCK
    blk = ATTN_BLOCK

    row = lax.broadcasted_iota(jnp.int32, (blk, 2 * blk), 0)
    col = lax.broadcasted_iota(jnp.int32, (blk, 2 * blk), 1)
    dist = blk + row - col
    band = (dist >= 0) & (dist <= blk)
    has_prev = jnp.broadcast_to(i > 0, band.shape)
    bias[1] = jnp.where(band, 0.0, NEG_INF)
    bias[0] = jnp.where(band & ((col >= blk) | has_prev), 0.0, NEG_INF)

    lane = lax.broadcasted_iota(jnp.int32, (blk, LANES), 1)
    ones = jnp.ones((2 * blk, HEAD_DIM), BF16)
    lse_ref[...] = jnp.zeros(lse_ref.shape, F32)

    def one_block(h, j):
        hc = pl.ds(pl.multiple_of(h * HEAD_DIM, HEAD_DIM), HEAD_DIM)
        rows = slice(j * blk, (j + 1) * blk)
        q = q_ref[rows, hc]
        if j == 0:
            kk = jnp.concatenate([kp_ref[:, hc], kc_ref[0:blk, hc]], axis=0)
            vv = jnp.concatenate([vp_ref[:, hc], vc_ref[0:blk, hc]], axis=0)
        else:
            kk = kc_ref[(j - 1) * blk:(j + 1) * blk, hc]
            vv = vc_ref[(j - 1) * blk:(j + 1) * blk, hc]
        s = lax.dot_general(q, kk, (((1,), (1,)), ((), ())), preferred_element_type=F32)
        s = s + bias[min(j, 1)]
        m = jnp.max(s, axis=-1, keepdims=True)
        p = jnp.exp2(s - m)
        ov = jnp.dot(p.astype(BF16), jnp.concatenate([vv, ones], axis=1),
                     preferred_element_type=F32)
        den = ov[:, HEAD_DIM:]
        o_ref[rows, hc] = (ov[:, :HEAD_DIM] / den).astype(o_ref.dtype)
        lse = m + jnp.log2(den)
        lse_ref[rows, :] = jnp.where(lane == h, lse, lse_ref[rows, :])

    def head_group(g, carry):
        for hh in range(heads_per_iter):
            for j in range(nblk):
                one_block(g * heads_per_iter + hh, j)
        return carry

    lax.fori_loop(0, n_heads // heads_per_iter, head_group, 0)


def _attn_branch(q, k, v):
    bsz, dil, length, d = q.shape
    n_heads = d // HEAD_DIM
    qb = min(length, ATTN_ROWS_PER_STEP)
    halo = length > qb
    per = qb // ATTN_BLOCK
    heads_per_iter = min(n_heads, max(1, ATTN_BLOCKS_PER_ITER // per))
    cur = lambda width: pl.BlockSpec((None, None, qb, width), lambda b, r, i: (b, r, i, 0))
    prev = pl.BlockSpec((None, None, ATTN_BLOCK, d),
                        lambda b, r, i: (b, r, jnp.maximum(i * per - 1, 0), 0))
    kern = functools.partial(_attn_kernel, qb=qb, n_heads=n_heads,
                             heads_per_iter=heads_per_iter, halo=halo)
    return pl.pallas_call(
        kern,
        grid=(bsz, dil, length // qb),
        in_specs=[cur(d), prev, cur(d), prev, cur(d)] if halo else [cur(d)] * 3,
        out_specs=[cur(d), cur(LANES)],
        out_shape=[
            jax.ShapeDtypeStruct((bsz, dil, length, d), BF16),
            jax.ShapeDtypeStruct((bsz, dil, length, LANES), F32),
        ],
        scratch_shapes=[pltpu.VMEM((2, ATTN_BLOCK, 2 * ATTN_BLOCK), F32)],
        compiler_params=pltpu.CompilerParams(
            dimension_semantics=("arbitrary", "arbitrary", "arbitrary"),
            vmem_limit_bytes=VMEM_LIMIT_BYTES),
        name=f"attn_d{dil}",
    )(*((q, k, k, v, v) if halo else (q, k, v)))


def _attn_out_kernel(o1_ref, o4_ref, o16_ref, l1_ref, l4_ref, l16_ref, h_ref, wo_ref, g_ref,
                     b_ref, out_ref, o_nat, z16, l_nat, zl16, ybuf, *xbufs, n_heads, nblocks):
    tm = h_ref.shape[0]
    g = pl.program_id(0)
    heads_per_dot = MXU_DIM // HEAD_DIM

    def to_natural(dst, src4):
        for r in range(4):
            dst[pl.ds(r, tm // 4, stride=4), :] = src4(r)

    def from_d16(tmp, src16):
        for r in range(4):
            for q in range(4):
                tmp[r, pl.ds(q, tm // 16, stride=4), :] = src16(4 * q + r)
        return lambda r: tmp[r]

    def merge_weights():
        to_natural(l_nat.at[0], lambda r: l4_ref[r])
        to_natural(l_nat.at[1], from_d16(zl16, lambda r: l16_ref[r]))
        lses = [l1_ref[0], l_nat[0], l_nat[1]]
        mx = functools.reduce(jnp.maximum, lses)
        es = [jnp.exp2(l - mx) for l in lses]
        tot = functools.reduce(lambda a, b: a + b, es)
        return es[0] / tot, es[1] / tot

    def merge_head(xbuf, w1, w4, h):
        hc = slice(h * HEAD_DIM, (h + 1) * HEAD_DIM)
        to_natural(o_nat.at[0, h], lambda r: o4_ref[r, :, hc].astype(F32))
        to_natural(o_nat.at[1, h],
                   from_d16(z16.at[h], lambda r: o16_ref[r, :, hc].astype(F32)))
        o1 = o1_ref[0, :, hc].astype(F32)
        o4 = o_nat[0, h]
        o16 = o_nat[1, h]
        shape = (tm, HEAD_DIM)
        o = (o16 + jnp.broadcast_to(w1[:, h:h + 1], shape) * (o1 - o16)
             + jnp.broadcast_to(w4[:, h:h + 1], shape) * (o4 - o16))
        xbuf[:, hc] = o.astype(BF16)

    def step(x_dst, x_src):
        if x_dst is not None:
            w1, w4 = merge_weights()
        for n in range(n_heads // heads_per_dot):
            if x_src is not None:
                cols = slice(n * MXU_DIM, (n + 1) * MXU_DIM)
                ybuf[:, cols] = jnp.dot(x_src[...], wo_ref[:, cols], preferred_element_type=F32)
            if x_dst is not None:
                for h in range(n * heads_per_dot, (n + 1) * heads_per_dot):
                    merge_head(x_dst, w1, w4, h)
        if x_src is not None:
            y = DEEPNORM_ALPHA * h_ref[...] + ybuf[...]
            out_ref[...] = _layer_norm(y, g_ref[...], b_ref[...])

    @pl.when(g == 0)
    def _():
        step(xbufs[0], None)

    for parity in range(2):
        @pl.when((g > 0) & (g < nblocks) & (g % 2 == parity))
        def _():
            step(xbufs[parity], xbufs[1 - parity])

    @pl.when(g == nblocks)
    def _():
        step(None, xbufs[(nblocks - 1) % 2])


def _attn_out(os, ls, h3d, w_o, ln_g, ln_b, *, tm=256):
    bsz, seq, d = h3d.shape
    assert DILATIONS == (1, 4, 16), "the two-step row interleave is written for 1, 4, 16"
    n_heads = d // HEAD_DIM
    blocks_per_seq = seq // tm
    nblocks = bsz * blocks_per_seq
    cur = _lagged_block(nblocks, blocks_per_seq, 0)
    prev = _lagged_block(nblocks, blocks_per_seq, 1)
    xbuf = pltpu.VMEM((tm, d), BF16)
    kern = functools.partial(_attn_out_kernel, n_heads=n_heads, nblocks=nblocks)
    return pl.pallas_call(
        kern,
        grid=(nblocks + 1,),
        in_specs=([_dilated_spec(dil, tm, d, cur) for dil in DILATIONS]
                  + [_dilated_spec(dil, tm, LANES, cur) for dil in DILATIONS]
                  + [_nat_spec(tm, d, prev), _const_spec((d, d)), _const_spec((1, d)),
                     _const_spec((1, d))]),
        out_specs=_nat_spec(tm, d, prev),
        out_shape=jax.ShapeDtypeStruct((bsz, seq, d), F32),
        scratch_shapes=[
            pltpu.VMEM((2, n_heads, tm, HEAD_DIM), F32),
            pltpu.VMEM((n_heads, 4, tm // 4, HEAD_DIM), F32),
            pltpu.VMEM((2, tm, LANES), F32),
            pltpu.VMEM((4, tm // 4, LANES), F32),
            pltpu.VMEM((tm, d), F32),
            xbuf, xbuf,
        ],
        compiler_params=pltpu.CompilerParams(
            dimension_semantics=("arbitrary",), vmem_limit_bytes=VMEM_LIMIT_BYTES),
        name="attn_out",
    )(*os, *ls, h3d, w_o, ln_g, ln_b)


def _cast_pad_rows_kernel(x_ref, o_ref):
    n = x_ref.shape[0]
    o_ref[0:n, :] = x_ref[...].astype(BF16)
    if o_ref.shape[0] > n:
        o_ref[n:, :] = jnp.zeros((o_ref.shape[0] - n, o_ref.shape[1]), BF16)


def _cast_chunks_kernel(x_ref, o_ref):
    nf, _, tf = o_ref.shape
    f = x_ref.shape[1]
    for c in range(nf):
        width = min(tf, f - c * tf)
        o_ref[c, :, :width] = x_ref[:, c * tf:c * tf + width].astype(BF16)
        if width < tf:
            o_ref[c, :, width:] = jnp.zeros((o_ref.shape[1], tf - width), BF16)


def _ffn_weights(w_up, w_down, tf, *, rows=256, cols=256):
    n_layers, d, two_f = w_up.shape
    f = two_f // 2
    fp = f + (-f % tf)
    nf = fp // tf
    params = pltpu.CompilerParams(dimension_semantics=("arbitrary",) * 3,
                                  vmem_limit_bytes=VMEM_LIMIT_BYTES)
    w_gv = pl.pallas_call(
        _cast_chunks_kernel,
        grid=(n_layers, d // rows, 2),
        in_specs=[pl.BlockSpec((None, rows, f), lambda l, r, j: (l, r, j))],
        out_specs=pl.BlockSpec((None, nf, None, rows, tf), lambda l, r, j: (l, 0, j, r, 0)),
        out_shape=jax.ShapeDtypeStruct((n_layers, nf, 2, d, tf), BF16),
        compiler_params=params,
        name="ffn_w_up_bf16",
    )(w_up)
    w_dn = pl.pallas_call(
        _cast_pad_rows_kernel,
        grid=(n_layers, d // cols, 1),
        in_specs=[pl.BlockSpec((None, f, cols), lambda l, c, _: (l, 0, c))],
        out_specs=pl.BlockSpec((None, fp, cols), lambda l, c, _: (l, 0, c)),
        out_shape=jax.ShapeDtypeStruct((n_layers, fp, d), BF16),
        compiler_params=params,
        name="ffn_w_down_bf16",
    )(w_down)
    return w_gv, w_dn, fp


def _conv_params(conv_w, conv_b, fp):
    two_f = conv_w.shape[1]
    f = two_f // 2
    fill = jnp.zeros((SUBLANES - CONV_WIDTH - 1, two_f), F32)
    cp = jnp.concatenate([conv_w, conv_b[None, :], fill], axis=0)
    z = jnp.zeros((SUBLANES, fp - f), F32)
    return jnp.concatenate([cp[:, :f], z, cp[:, f:], z], axis=1)


def kernel(x, pool_w_in, pool_w_grp, pool_scale, pool_w_out, attn_w_q, attn_w_o, shared_w_k,
           shared_w_v, ffn_w_up, ffn_conv_w, ffn_conv_b, ffn_w_down, ln1_g, ln1_b, ln2_g, ln2_b):
    bsz, seq, d = x.shape
    n_a = pool_w_in.shape[0]
    n_layers = ffn_w_up.shape[0]
    tf = 2 * MXU_DIM
    row = lambda a: a.reshape(1, d)

    w_gv, w_dn, fp = _ffn_weights(ffn_w_up, ffn_w_down, tf)
    h = x.reshape(bsz * seq, d)
    h_bf = None
    kv = None
    for i in range(n_layers):
        if i < n_a:
            h = _pool_layer(h, pool_w_in[i].astype(BF16), pool_w_grp[i].astype(BF16),
                            row(pool_scale[i]), pool_w_out[i].astype(BF16),
                            row(ln1_g[i]), row(ln1_b[i]), seq=seq)
        else:
            h3d = h.reshape(bsz, seq, d)
            x3d = h3d if h_bf is None else h_bf.reshape(bsz, seq, d)
            if kv is None:
                kv = (_proj(x3d, shared_w_k.astype(BF16)), _proj(x3d, shared_w_v.astype(BF16)))
            qs = _proj(x3d, attn_w_q[i - n_a].astype(BF16), out_scale=QK_SCALE_LOG2)
            outs = [_attn_branch(q, k, v) for q, k, v in zip(qs, kv[0], kv[1])]
            h = _attn_out([o for o, _ in outs], [l for _, l in outs], h3d,
                          attn_w_o[i - n_a].astype(BF16), row(ln1_g[i]), row(ln1_b[i]))
            h = h.reshape(bsz * seq, d)
        cp = _conv_params(ffn_conv_w[i], ffn_conv_b[i], fp)
        feeds_attention = n_a <= i + 1 < n_layers
        outs = _ffn_layer(h, w_gv, cp, w_dn, row(ln2_g[i]), row(ln2_b[i]), layer=i, seq=seq,
                          bf16_copy=feeds_attention)
        h, h_bf = (outs[0], outs[1]) if feeds_attention else (outs[0], None)
    return h.reshape(bsz, seq, d)
```

```python
import functools
import math

import jax
import jax.numpy as jnp
from jax import lax
from jax.experimental import pallas as pl
from jax.experimental.pallas import tpu as pltpu

POOL_WINDOWS = (2, 4, 8, 16)
HEAD_DIM = 128
DILATED_BRANCHES = ((128, 1), (512, 4), (2048, 16))
ATTN_BLOCK = 128
CONV_WIDTH = 3
DEPTH = 2
DEEPNORM_ALPHA = (2.0 * DEPTH) ** 0.25
LN_EPS = 1e-5
NEG_INF = -1e30

LANES = 128
SUBLANES = 8
MXU_DIM = 256
VMEM_LIMIT_BYTES = 56 * 1024 * 1024

DILATIONS = tuple(d for _, d in DILATED_BRANCHES)
ATTN_ROWS_PER_STEP = 8 * ATTN_BLOCK
ATTN_BLOCKS_PER_ITER = 128
QK_SCALE_LOG2 = math.log2(math.e) / math.sqrt(HEAD_DIM)
POOL_HALO = max(POOL_WINDOWS)

F32 = jnp.float32
BF16 = jnp.bfloat16


def _layer_norm(y, g, b):
    mu = jnp.mean(y, axis=-1, keepdims=True)
    d = y - mu
    var = jnp.mean(d * d, axis=-1, keepdims=True)
    return d * lax.rsqrt(var + LN_EPS) * g + b


def _const_spec(shape):
    nd = len(shape)
    return pl.BlockSpec(shape, lambda *_: (0,) * nd, pipeline_mode=pl.Buffered(1))


def _pool_kernel(x_ref, win_ref, wgrp_ref, scale_ref, wout_ref, g_ref, b_ref, o_ref,
                 pbuf, mbuf, *, tm, blocks_per_seq):
    i = pl.program_id(0)
    blk = i % blocks_per_seq
    d_model = x_ref.shape[1]
    gdim = d_model // len(POOL_WINDOWS)

    @pl.when(blk == 0)
    def _():
        pbuf[0:POOL_HALO, :] = jnp.zeros((POOL_HALO, d_model), F32)

    @pl.when(blk != 0)
    def _():
        pbuf[0:POOL_HALO, :] = pbuf[tm:tm + POOL_HALO, :]

    x = x_ref[...]
    p = jnp.dot(x.astype(BF16), win_ref[...], preferred_element_type=F32)
    pbuf[POOL_HALO:POOL_HALO + tm, :] = p

    pos = blk * tm + lax.broadcasted_iota(jnp.int32, (tm, 1), 0)
    for g, w in enumerate(POOL_WINDOWS):
        cols = slice(g * gdim, (g + 1) * gdim)
        s = pbuf[:, cols]
        shift = 1
        while shift < w:
            s = s + pltpu.roll(s, shift, axis=0)
            shift *= 2
        cnt = jnp.minimum(pos + 1, w).astype(F32)
        pooled = s[POOL_HALO:, :] / cnt - p[:, cols]
        mixed = jnp.dot(pooled.astype(BF16), wgrp_ref[g], preferred_element_type=F32)
        mbuf[:, cols] = (mixed * scale_ref[:, cols]).astype(BF16)

    mix = jnp.dot(mbuf[...], wout_ref[...], preferred_element_type=F32)
    o_ref[...] = _layer_norm(DEEPNORM_ALPHA * x + mix, g_ref[...], b_ref[...])


def _pool_layer(x2d, w_in, w_grp, scale, w_out, ln_g, ln_b, *, seq, tm=512):
    t, d = x2d.shape
    ng, gdim, _ = w_grp.shape
    kern = functools.partial(_pool_kernel, tm=tm, blocks_per_seq=seq // tm)
    return pl.pallas_call(
        kern,
        grid=(t // tm,),
        in_specs=[
            pl.BlockSpec((tm, d), lambda i: (i, 0)),
            _const_spec((d, d)),
            _const_spec((ng, gdim, gdim)),
            _const_spec((1, d)),
            _const_spec((d, d)),
            _const_spec((1, d)),
            _const_spec((1, d)),
        ],
        out_specs=pl.BlockSpec((tm, d), lambda i: (i, 0)),
        out_shape=jax.ShapeDtypeStruct((t, d), F32),
        scratch_shapes=[
            pltpu.VMEM((POOL_HALO + tm, d), F32),
            pltpu.VMEM((tm, d), BF16),
        ],
        compiler_params=pltpu.CompilerParams(
            dimension_semantics=("arbitrary",), vmem_limit_bytes=VMEM_LIMIT_BYTES),
        name="pool_layer",
    )(x2d, w_in, w_grp, scale, w_out, ln_g, ln_b)


def _ffn_kernel(h_ref, wgv_ref, cp_ref, wd_ref, g_ref, b_ref, *refs,
                tm, nf, blocks_per_seq, n_out):
    out_refs = refs[:n_out]
    hb, acc, ug_buf, uv_buf, carry_g, carry_v, *act_bufs = refs[n_out:]
    i = pl.program_id(0)
    f = pl.program_id(1)
    seq_start = (i % blocks_per_seq) == 0
    wg_ref, wv_ref = wgv_ref.at[0], wgv_ref.at[1]
    tf = wgv_ref.shape[2]
    fp = nf * tf

    def conv(uc, buf, carry, cp_base, c):
        cs = slice(c * LANES, (c + 1) * LANES)
        cp = cp_ref[:, pl.ds(pl.multiple_of(cp_base + f * tf + c * LANES, LANES), LANES)]
        buf[c, 0:SUBLANES, :] = jnp.where(seq_start, 0.0, carry[f, :, cs])
        buf[c, SUBLANES:SUBLANES + tm, :] = uc
        carry[f, :, cs] = uc[tm - SUBLANES:tm, :]
        out = cp[CONV_WIDTH:CONV_WIDTH + 1, :] + uc * cp[CONV_WIDTH - 1:CONV_WIDTH, :]
        for lag in range(1, CONV_WIDTH):
            tap = CONV_WIDTH - 1 - lag
            out = out + buf[c, SUBLANES - lag:SUBLANES - lag + tm, :] * cp[tap:tap + 1, :]
        return out

    def up_stage(act_dst):
        ug = jnp.dot(hb[...], wg_ref[...], preferred_element_type=F32)
        uv = jnp.dot(hb[...], wv_ref[...], preferred_element_type=F32)
        for c in range(tf // LANES):
            cs = slice(c * LANES, (c + 1) * LANES)
            gate = conv(ug[:, cs], ug_buf, carry_g, 0, c)
            val = conv(uv[:, cs], uv_buf, carry_v, fp, c)
            sig_den = 1.0 + jnp.exp2(gate * (-math.log2(math.e)))
            act_dst[:, cs] = (gate / sig_den * val).astype(BF16)

    def down_stage(act_src):
        acc[...] += jnp.dot(act_src[...], wd_ref[...], preferred_element_type=F32)

    @pl.when(f == 0)
    def _():
        hb[...] = h_ref[...].astype(BF16)
        acc[...] = jnp.zeros(acc.shape, F32)
        up_stage(act_bufs[0])

    for parity in range(2):
        @pl.when((f > 0) & (f < nf) & (f % 2 == parity))
        def _():
            up_stage(act_bufs[parity])
            down_stage(act_bufs[1 - parity])

    @pl.when(f == nf)
    def _():
        act_src = act_bufs[(nf - 1) % 2]
        half = tm // 2
        for rows in (slice(0, half), slice(half, tm)):
            mix = acc[rows, :] + jnp.dot(act_src[rows, :], wd_ref[...],
                                         preferred_element_type=F32)
            out = _layer_norm(DEEPNORM_ALPHA * h_ref[rows, :] + mix, g_ref[...], b_ref[...])
            for o_ref in out_refs:
                o_ref[rows, :] = out.astype(o_ref.dtype)


def _ffn_layer(h2d, w_gv, cp, w_down, ln_g, ln_b, *, layer, seq, bf16_copy, tm=512):
    t, d = h2d.shape
    nf, tf = w_gv.shape[1], w_gv.shape[4]
    fp = nf * tf
    out_dtypes = (F32, BF16) if bf16_copy else (F32,)
    kern = functools.partial(_ffn_kernel, tm=tm, nf=nf, blocks_per_seq=seq // tm,
                             n_out=len(out_dtypes))
    return pl.pallas_call(
        kern,
        grid=(t // tm, nf + 1),
        in_specs=[
            pl.BlockSpec((tm, d), lambda i, f: (i, 0)),
            pl.BlockSpec((None, None, 2, d, tf),
                         lambda i, f: (layer, jnp.minimum(f, nf - 1), 0, 0, 0)),
            _const_spec((SUBLANES, 2 * fp)),
            pl.BlockSpec((None, tf, d), lambda i, f: (layer, jnp.maximum(f - 1, 0), 0)),
            _const_spec((1, d)),
            _const_spec((1, d)),
        ],
        out_specs=[pl.BlockSpec((tm, d), lambda i, f: (i, 0)) for _ in out_dtypes],
        out_shape=[jax.ShapeDtypeStruct((t, d), dt) for dt in out_dtypes],
        scratch_shapes=[
            pltpu.VMEM((tm, d), BF16),
            pltpu.VMEM((tm, d), F32),
            pltpu.VMEM((tf // LANES, SUBLANES + tm, LANES), F32),
            pltpu.VMEM((tf // LANES, SUBLANES + tm, LANES), F32),
            pltpu.VMEM((nf, SUBLANES, tf), F32),
            pltpu.VMEM((nf, SUBLANES, tf), F32),
            pltpu.VMEM((tm, tf), BF16),
            pltpu.VMEM((tm, tf), BF16),
        ],
        compiler_params=pltpu.CompilerParams(
            dimension_semantics=("arbitrary", "arbitrary"), vmem_limit_bytes=VMEM_LIMIT_BYTES),
        name="ffn_layer",
    )(h2d, w_gv, cp, w_down, ln_g, ln_b)


def _lagged_block(nblocks, blocks_per_seq, lag):
    def split(g):
        j = jnp.clip(g - lag, 0, nblocks - 1)
        return j // blocks_per_seq, j % blocks_per_seq
    return split


def _nat_spec(tm, width, split):
    def idx(g):
        b, i = split(g)
        return (b, i, 0)
    return pl.BlockSpec((None, tm, width), idx)


def _dilated_spec(dil, tm, width, split):
    def idx(g):
        b, i = split(g)
        return (b, 0, i, 0)
    return pl.BlockSpec((None, dil, tm // dil, width), idx)


def _proj_kernel(x_ref, w_ref, *refs, nblocks, out_scale):
    nat_ref, o4_ref, o16_ref = refs[:3]
    ybufs, zbuf = refs[3:5], refs[5]
    n_tiles, tm, _ = ybufs[0].shape
    g = pl.program_id(0)
    tiles_per_dot = MXU_DIM // LANES

    def matmul(ybuf, xb, n):
        cols = slice(n * MXU_DIM, (n + 1) * MXU_DIM)
        y = jnp.dot(xb, w_ref[:, cols], preferred_element_type=F32)
        if out_scale != 1.0:
            y = y * out_scale
        nat_ref[0, :, cols] = y.astype(BF16)
        for t in range(tiles_per_dot):
            ybuf[n * tiles_per_dot + t] = y[:, t * LANES:(t + 1) * LANES]

    def scatter(ybuf, n):
        for c in range(n * tiles_per_dot, (n + 1) * tiles_per_dot):
            cs = slice(c * LANES, (c + 1) * LANES)
            for r in range(4):
                slab = ybuf[c, pl.ds(r, tm // 4, stride=4), :]
                o4_ref[r, :, cs] = slab.astype(BF16)
                zbuf[c, r] = slab
            for r in range(4):
                for q in range(4):
                    o16_ref[4 * q + r, :, cs] = (
                        zbuf[c, r, pl.ds(q, tm // 16, stride=4), :].astype(BF16))

    def step(y_dst, y_src):
        xb = None if y_dst is None else x_ref[...].astype(BF16)
        for n in range(n_tiles // tiles_per_dot):
            if y_dst is not None:
                matmul(y_dst, xb, n)
            if y_src is not None:
                scatter(y_src, n)

    @pl.when(g == 0)
    def _():
        step(ybufs[0], None)

    for parity in range(2):
        @pl.when((g > 0) & (g < nblocks) & (g % 2 == parity))
        def _():
            step(ybufs[parity], ybufs[1 - parity])

    @pl.when(g == nblocks)
    def _():
        step(None, ybufs[(nblocks - 1) % 2])


def _proj(h3d, w, *, out_scale=1.0, tm=512):
    bsz, seq, d = h3d.shape
    blocks_per_seq = seq // tm
    nblocks = bsz * blocks_per_seq
    cur = _lagged_block(nblocks, blocks_per_seq, 0)
    prev = _lagged_block(nblocks, blocks_per_seq, 1)
    assert DILATIONS == (1, 4, 16), "the two-step row de-interleave is written for 1, 4, 16"
    ybuf = pltpu.VMEM((d // LANES, tm, LANES), F32)
    zbuf = pltpu.VMEM((d // LANES, 4, tm // 4, LANES), F32)
    return pl.pallas_call(
        functools.partial(_proj_kernel, nblocks=nblocks, out_scale=out_scale),
        grid=(nblocks + 1,),
        in_specs=[_nat_spec(tm, d, cur), _const_spec((d, d))],
        out_specs=[_dilated_spec(1, tm, d, cur), _dilated_spec(4, tm, d, prev),
                   _dilated_spec(16, tm, d, prev)],
        out_shape=[jax.ShapeDtypeStruct((bsz, dil, seq // dil, d), BF16) for dil in DILATIONS],
        scratch_shapes=[ybuf, ybuf, zbuf],
        compiler_params=pltpu.CompilerParams(
            dimension_semantics=("arbitrary",), vmem_limit_bytes=VMEM_LIMIT_BYTES),
        name="proj",
    )(h3d, w)


def _attn_kernel(q_ref, *refs, qb, n_heads, heads_per_iter, halo):
    if halo:
        kp_ref, kc_ref, vp_ref, vc_ref, o_ref, lse_ref, bias = refs
    else:
        kc_ref, vc_ref, o_ref, lse_ref, bias = refs
        kp_ref, vp_ref = kc_ref.at[0:ATTN_BLOCK], vc_ref.at[0:ATTN_BLOCK]
    i = pl.program_id(2)
    nblk = qb // ATTN_BLOCK
    blk = ATTN_BLOCK

    row = lax.broadcasted_iota(jnp.int32, (blk, 2 * blk), 0)
    col = lax.broadcasted_iota(jnp.int32, (blk, 2 * blk), 1)
    dist = blk + row - col
    band = (dist >= 0) & (dist <= blk)
    has_prev = jnp.broadcast_to(i > 0, band.shape)
    bias[1] = jnp.where(band, 0.0, NEG_INF)
    bias[0] = jnp.where(band & ((col >= blk) | has_prev), 0.0, NEG_INF)

    lane = lax.broadcasted_iota(jnp.int32, (blk, LANES), 1)
    ones = jnp.ones((2 * blk, HEAD_DIM), BF16)
    lse_ref[...] = jnp.zeros(lse_ref.shape, F32)

    def one_block(h, j):
        hc = pl.ds(pl.multiple_of(h * HEAD_DIM, HEAD_DIM), HEAD_DIM)
        rows = slice(j * blk, (j + 1) * blk)
        q = q_ref[rows, hc]
        if j == 0:
            kk = jnp.concatenate([kp_ref[:, hc], kc_ref[0:blk, hc]], axis=0)
            vv = jnp.concatenate([vp_ref[:, hc], vc_ref[0:blk, hc]], axis=0)
        else:
            kk = kc_ref[(j - 1) * blk:(j + 1) * blk, hc]
            vv = vc_ref[(j - 1) * blk:(j + 1) * blk, hc]
        s = lax.dot_general(q, kk, (((1,), (1,)), ((), ())), preferred_element_type=F32)
        s = s + bias[min(j, 1)]
        m = jnp.max(s, axis=-1, keepdims=True)
        p = jnp.exp2(s - m)
        ov = jnp.dot(p.astype(BF16), jnp.concatenate([vv, ones], axis=1),
                     preferred_element_type=F32)
        den = ov[:, HEAD_DIM:]
        o_ref[rows, hc] = (ov[:, :HEAD_DIM] / den).astype(o_ref.dtype)
        lse = m + jnp.log2(den)
        lse_ref[rows, :] = jnp.where(lane == h, lse, lse_ref[rows, :])

    def head_group(g, carry):
        for hh in range(heads_per_iter):
            for j in range(nblk):
                one_block(g * heads_per_iter + hh, j)
        return carry

    lax.fori_loop(0, n_heads // heads_per_iter, head_group, 0)


def _attn_branch(q, k, v):
    bsz, dil, length, d = q.shape
    n_heads = d // HEAD_DIM
    qb = min(length, ATTN_ROWS_PER_STEP)
    halo = length > qb
    per = qb // ATTN_BLOCK
    heads_per_iter = min(n_heads, max(1, ATTN_BLOCKS_PER_ITER // per))
    cur = lambda width: pl.BlockSpec((None, None, qb, width), lambda b, r, i: (b, r, i, 0))
    prev = pl.BlockSpec((None, None, ATTN_BLOCK, d),
                        lambda b, r, i: (b, r, jnp.maximum(i * per - 1, 0), 0))
    kern = functools.partial(_attn_kernel, qb=qb, n_heads=n_heads,
                             heads_per_iter=heads_per_iter, halo=halo)
    return pl.pallas_call(
        kern,
        grid=(bsz, dil, length // qb),
        in_specs=[cur(d), prev, cur(d), prev, cur(d)] if halo else [cur(d)] * 3,
        out_specs=[cur(d), cur(LANES)],
        out_shape=[
            jax.ShapeDtypeStruct((bsz, dil, length, d), BF16),
            jax.ShapeDtypeStruct((bsz, dil, length, LANES), F32),
        ],
        scratch_shapes=[pltpu.VMEM((2, ATTN_BLOCK, 2 * ATTN_BLOCK), F32)],
        compiler_params=pltpu.CompilerParams(
            dimension_semantics=("arbitrary", "arbitrary", "arbitrary"),
            vmem_limit_bytes=VMEM_LIMIT_BYTES),
        name=f"attn_d{dil}",
    )(*((q, k, k, v, v) if halo else (q, k, v)))


def _attn_out_kernel(o1_ref, o4_ref, o16_ref, l1_ref, l4_ref, l16_ref, h_ref, wo_ref, g_ref,
                     b_ref, out_ref, o_nat, z16, l_nat, zl16, ybuf, *xbufs, n_heads, nblocks):
    tm = h_ref.shape[0]
    g = pl.program_id(0)
    heads_per_dot = MXU_DIM // HEAD_DIM

    def to_natural(dst, src4):
        for r in range(4):
            dst[pl.ds(r, tm // 4, stride=4), :] = src4(r)

    def from_d16(tmp, src16):
        for r in range(4):
            for q in range(4):
                tmp[r, pl.ds(q, tm // 16, stride=4), :] = src16(4 * q + r)
        return lambda r: tmp[r]

    def merge_weights():
        to_natural(l_nat.at[0], lambda r: l4_ref[r])
        to_natural(l_nat.at[1], from_d16(zl16, lambda r: l16_ref[r]))
        lses = [l1_ref[0], l_nat[0], l_nat[1]]
        mx = functools.reduce(jnp.maximum, lses)
        es = [jnp.exp2(l - mx) for l in lses]
        tot = functools.reduce(lambda a, b: a + b, es)
        return es[0] / tot, es[1] / tot

    def merge_head(xbuf, w1, w4, h):
        hc = slice(h * HEAD_DIM, (h + 1) * HEAD_DIM)
        to_natural(o_nat.at[0, h], lambda r: o4_ref[r, :, hc].astype(F32))
        to_natural(o_nat.at[1, h],
                   from_d16(z16.at[h], lambda r: o16_ref[r, :, hc].astype(F32)))
        o1 = o1_ref[0, :, hc].astype(F32)
        o4 = o_nat[0, h]
        o16 = o_nat[1, h]
        shape = (tm, HEAD_DIM)
        o = (o16 + jnp.broadcast_to(w1[:, h:h + 1], shape) * (o1 - o16)
             + jnp.broadcast_to(w4[:, h:h + 1], shape) * (o4 - o16))
        xbuf[:, hc] = o.astype(BF16)

    def step(x_dst, x_src):
        if x_dst is not None:
            w1, w4 = merge_weights()
        for n in range(n_heads // heads_per_dot):
            if x_src is not None:
                cols = slice(n * MXU_DIM, (n + 1) * MXU_DIM)
                ybuf[:, cols] = jnp.dot(x_src[...], wo_ref[:, cols], preferred_element_type=F32)
            if x_dst is not None:
                for h in range(n * heads_per_dot, (n + 1) * heads_per_dot):
                    merge_head(x_dst, w1, w4, h)
        if x_src is not None:
            y = DEEPNORM_ALPHA * h_ref[...] + ybuf[...]
            out_ref[...] = _layer_norm(y, g_ref[...], b_ref[...])

    @pl.when(g == 0)
    def _():
        step(xbufs[0], None)

    for parity in range(2):
        @pl.when((g > 0) & (g < nblocks) & (g % 2 == parity))
        def _():
            step(xbufs[parity], xbufs[1 - parity])

    @pl.when(g == nblocks)
    def _():
        step(None, xbufs[(nblocks - 1) % 2])


def _attn_out(os, ls, h3d, w_o, ln_g, ln_b, *, tm=256):
    bsz, seq, d = h3d.shape
    assert DILATIONS == (1, 4, 16), "the two-step row interleave is written for 1, 4, 16"
    n_heads = d // HEAD_DIM
    blocks_per_seq = seq // tm
    nblocks = bsz * blocks_per_seq
    cur = _lagged_block(nblocks, blocks_per_seq, 0)
    prev = _lagged_block(nblocks, blocks_per_seq, 1)
    xbuf = pltpu.VMEM((tm, d), BF16)
    kern = functools.partial(_attn_out_kernel, n_heads=n_heads, nblocks=nblocks)
    return pl.pallas_call(
        kern,
        grid=(nblocks + 1,),
        in_specs=([_dilated_spec(dil, tm, d, cur) for dil in DILATIONS]
                  + [_dilated_spec(dil, tm, LANES, cur) for dil in DILATIONS]
                  + [_nat_spec(tm, d, prev), _const_spec((d, d)), _const_spec((1, d)),
                     _const_spec((1, d))]),
        out_specs=_nat_spec(tm, d, prev),
        out_shape=jax.ShapeDtypeStruct((bsz, seq, d), F32),
        scratch_shapes=[
            pltpu.VMEM((2, n_heads, tm, HEAD_DIM), F32),
            pltpu.VMEM((n_heads, 4, tm // 4, HEAD_DIM), F32),
            pltpu.VMEM((2, tm, LANES), F32),
            pltpu.VMEM((4, tm // 4, LANES), F32),
            pltpu.VMEM((tm, d), F32),
            xbuf, xbuf,
        ],
        compiler_params=pltpu.CompilerParams(
            dimension_semantics=("arbitrary",), vmem_limit_bytes=VMEM_LIMIT_BYTES),
        name="attn_out",
    )(*os, *ls, h3d, w_o, ln_g, ln_b)


def _cast_pad_rows_kernel(x_ref, o_ref):
    n = x_ref.shape[0]
    o_ref[0:n, :] = x_ref[...].astype(BF16)
    if o_ref.shape[0] > n:
        o_ref[n:, :] = jnp.zeros((o_ref.shape[0] - n, o_ref.shape[1]), BF16)


def _cast_chunks_kernel(x_ref, o_ref):
    nf, _, tf = o_ref.shape
    f = x_ref.shape[1]
    for c in range(nf):
        width = min(tf, f - c * tf)
        o_ref[c, :, :width] = x_ref[:, c * tf:c * tf + width].astype(BF16)
        if width < tf:
            o_ref[c, :, width:] = jnp.zeros((o_ref.shape[1], tf - width), BF16)


def _ffn_weights(w_up, w_down, tf, *, rows=256, cols=256):
    n_layers, d, two_f = w_up.shape
    f = two_f // 2
    fp = f + (-f % tf)
    nf = fp // tf
    params = pltpu.CompilerParams(dimension_semantics=("arbitrary",) * 3,
                                  vmem_limit_bytes=VMEM_LIMIT_BYTES)
    w_gv = pl.pallas_call(
        _cast_chunks_kernel,
        grid=(n_layers, d // rows, 2),
        in_specs=[pl.BlockSpec((None, rows, f), lambda l, r, j: (l, r, j))],
        out_specs=pl.BlockSpec((None, nf, None, rows, tf), lambda l, r, j: (l, 0, j, r, 0)),
        out_shape=jax.ShapeDtypeStruct((n_layers, nf, 2, d, tf), BF16),
        compiler_params=params,
        name="ffn_w_up_bf16",
    )(w_up)
    w_dn = pl.pallas_call(
        _cast_pad_rows_kernel,
        grid=(n_layers, d // cols, 1),
        in_specs=[pl.BlockSpec((None, f, cols), lambda l, c, _: (l, 0, c))],
        out_specs=pl.BlockSpec((None, fp, cols), lambda l, c, _: (l, 0, c)),
        out_shape=jax.ShapeDtypeStruct((n_layers, fp, d), BF16),
        compiler_params=params,
        name="ffn_w_down_bf16",
    )(w_down)
    return w_gv, w_dn, fp


def _conv_params(conv_w, conv_b, fp):
    two_f = conv_w.shape[1]
    f = two_f // 2
    fill = jnp.zeros((SUBLANES - CONV_WIDTH - 1, two_f), F32)
    cp = jnp.concatenate([conv_w, conv_b[None, :], fill], axis=0)
    z = jnp.zeros((SUBLANES, fp - f), F32)
    return jnp.concatenate([cp[:, :f], z, cp[:, f:], z], axis=1)


def kernel(x, pool_w_in, pool_w_grp, pool_scale, pool_w_out, attn_w_q, attn_w_o, shared_w_k,
           shared_w_v, ffn_w_up, ffn_conv_w, ffn_conv_b, ffn_w_down, ln1_g, ln1_b, ln2_g, ln2_b):
    bsz, seq, d = x.shape
    n_a = pool_w_in.shape[0]
    n_layers = ffn_w_up.shape[0]
    tf = 2 * MXU_DIM
    row = lambda a: a.reshape(1, d)

    w_gv, w_dn, fp = _ffn_weights(ffn_w_up, ffn_w_down, tf)
    h = x.reshape(bsz * seq, d)
    h_bf = None
    kv = None
    for i in range(n_layers):
        if i < n_a:
            h = _pool_layer(h, pool_w_in[i].astype(BF16), pool_w_grp[i].astype(BF16),
                            row(pool_scale[i]), pool_w_out[i].astype(BF16),
                            row(ln1_g[i]), row(ln1_b[i]), seq=seq)
        else:
            h3d = h.reshape(bsz, seq, d)
            x3d = h3d if h_bf is None else h_bf.reshape(bsz, seq, d)
            if kv is None:
                kv = (_proj(x3d, shared_w_k.astype(BF16)), _proj(x3d, shared_w_v.astype(BF16)))
            qs = _proj(x3d, attn_w_q[i - n_a].astype(BF16), out_scale=QK_SCALE_LOG2)
            outs = [_attn_branch(q, k, v) for q, k, v in zip(qs, kv[0], kv[1])]
            h = _attn_out([o for o, _ in outs], [l for _, l in outs], h3d,
                          attn_w_o[i - n_a].astype(BF16), row(ln1_g[i]), row(ln1_b[i]))
            h = h.reshape(bsz * seq, d)
        cp = _conv_params(ffn_conv_w[i], ffn_conv_b[i], fp)
        feeds_attention = n_a <= i + 1 < n_layers
        outs = _ffn_layer(h, w_gv, cp, w_dn, row(ln2_g[i]), row(ln2_b[i]), layer=i, seq=seq,
                          bf16_copy=feeds_attention)
        h, h_bf = (outs[0], outs[1]) if feeds_attention else (outs[0], None)
    return h.reshape(bsz, seq, d)
```

```python
import functools
import math

import jax
import jax.numpy as jnp
from jax import lax
from jax.experimental import pallas as pl
from jax.experimental.pallas import tpu as pltpu

POOL_WINDOWS = (2, 4, 8, 16)
HEAD_DIM = 128
DILATED_BRANCHES = ((128, 1), (512, 4), (2048, 16))
ATTN_BLOCK = 128
CONV_WIDTH = 3
DEPTH = 2
DEEPNORM_ALPHA = (2.0 * DEPTH) ** 0.25
LN_EPS = 1e-5
NEG_INF = -1e30

LANES = 128
SUBLANES = 8
MXU_DIM = 256
VMEM_LIMIT_BYTES = 56 * 1024 * 1024

DILATIONS = tuple(d for _, d in DILATED_BRANCHES)
ATTN_ROWS_PER_STEP = 8 * ATTN_BLOCK
ATTN_BLOCKS_PER_ITER = 128
QK_SCALE_LOG2 = math.log2(math.e) / math.sqrt(HEAD_DIM)
POOL_HALO = max(POOL_WINDOWS)

F32 = jnp.float32
BF16 = jnp.bfloat16


def _layer_norm(y, g, b):
    mu = jnp.mean(y, axis=-1, keepdims=True)
    d = y - mu
    var = jnp.mean(d * d, axis=-1, keepdims=True)
    return d * lax.rsqrt(var + LN_EPS) * g + b


def _const_spec(shape):
    nd = len(shape)
    return pl.BlockSpec(shape, lambda *_: (0,) * nd, pipeline_mode=pl.Buffered(1))


def _pool_kernel(x_ref, win_ref, wgrp_ref, scale_ref, wout_ref, g_ref, b_ref, o_ref,
                 pbuf, mbuf, *, tm, blocks_per_seq):
    i = pl.program_id(0)
    blk = i % blocks_per_seq
    d_model = x_ref.shape[1]
    gdim = d_model // len(POOL_WINDOWS)

    @pl.when(blk == 0)
    def _():
        pbuf[0:POOL_HALO, :] = jnp.zeros((POOL_HALO, d_model), F32)

    @pl.when(blk != 0)
    def _():
        pbuf[0:POOL_HALO, :] = pbuf[tm:tm + POOL_HALO, :]

    x = x_ref[...]
    p = jnp.dot(x.astype(BF16), win_ref[...], preferred_element_type=F32)
    pbuf[POOL_HALO:POOL_HALO + tm, :] = p

    pos = blk * tm + lax.broadcasted_iota(jnp.int32, (tm, 1), 0)
    for g, w in enumerate(POOL_WINDOWS):
        cols = slice(g * gdim, (g + 1) * gdim)
        s = pbuf[:, cols]
        shift = 1
        while shift < w:
            s = s + pltpu.roll(s, shift, axis=0)
            shift *= 2
        cnt = jnp.minimum(pos + 1, w).astype(F32)
        pooled = s[POOL_HALO:, :] / cnt - p[:, cols]
        mixed = jnp.dot(pooled.astype(BF16), wgrp_ref[g], preferred_element_type=F32)
        mbuf[:, cols] = (mixed * scale_ref[:, cols]).astype(BF16)

    mix = jnp.dot(mbuf[...], wout_ref[...], preferred_element_type=F32)
    o_ref[...] = _layer_norm(DEEPNORM_ALPHA * x + mix, g_ref[...], b_ref[...])


def _pool_layer(x2d, w_in, w_grp, scale, w_out, ln_g, ln_b, *, seq, tm=512):
    t, d = x2d.shape
    ng, gdim, _ = w_grp.shape
    kern = functools.partial(_pool_kernel, tm=tm, blocks_per_seq=seq // tm)
    return pl.pallas_call(
        kern,
        grid=(t // tm,),
        in_specs=[
            pl.BlockSpec((tm, d), lambda i: (i, 0)),
            _const_spec((d, d)),
            _const_spec((ng, gdim, gdim)),
            _const_spec((1, d)),
            _const_spec((d, d)),
            _const_spec((1, d)),
            _const_spec((1, d)),
        ],
        out_specs=pl.BlockSpec((tm, d), lambda i: (i, 0)),
        out_shape=jax.ShapeDtypeStruct((t, d), F32),
        scratch_shapes=[
            pltpu.VMEM((POOL_HALO + tm, d), F32),
            pltpu.VMEM((tm, d), BF16),
        ],
        compiler_params=pltpu.CompilerParams(
            dimension_semantics=("arbitrary",), vmem_limit_bytes=VMEM_LIMIT_BYTES),
        name="pool_layer",
    )(x2d, w_in, w_grp, scale, w_out, ln_g, ln_b)


def _ffn_kernel(h_ref, wgv_ref, cp_ref, wd_ref, g_ref, b_ref, *refs,
                tm, nf, blocks_per_seq, n_out):
    out_refs = refs[:n_out]
    hb, acc, ug_buf, uv_buf, carry_g, carry_v, *act_bufs = refs[n_out:]
    i = pl.program_id(0)
    f = pl.program_id(1)
    seq_start = (i % blocks_per_seq) == 0
    wg_ref, wv_ref = wgv_ref.at[0], wgv_ref.at[1]
    tf = wgv_ref.shape[2]
    fp = nf * tf

    def conv(uc, buf, carry, cp_base, c):
        cs = slice(c * LANES, (c + 1) * LANES)
        cp = cp_ref[:, pl.ds(pl.multiple_of(cp_base + f * tf + c * LANES, LANES), LANES)]
        buf[c, 0:SUBLANES, :] = jnp.where(seq_start, 0.0, carry[f, :, cs])
        buf[c, SUBLANES:SUBLANES + tm, :] = uc
        carry[f, :, cs] = uc[tm - SUBLANES:tm, :]
        out = cp[CONV_WIDTH:CONV_WIDTH + 1, :] + uc * cp[CONV_WIDTH - 1:CONV_WIDTH, :]
        for lag in range(1, CONV_WIDTH):
            tap = CONV_WIDTH - 1 - lag
            out = out + buf[c, SUBLANES - lag:SUBLANES - lag + tm, :] * cp[tap:tap + 1, :]
        return out

    def up_stage(act_dst):
        ug = jnp.dot(hb[...], wg_ref[...], preferred_element_type=F32)
        uv = jnp.dot(hb[...], wv_ref[...], preferred_element_type=F32)
        for c in range(tf // LANES):
            cs = slice(c * LANES, (c + 1) * LANES)
            gate = conv(ug[:, cs], ug_buf, carry_g, 0, c)
            val = conv(uv[:, cs], uv_buf, carry_v, fp, c)
            sig_den = 1.0 + jnp.exp2(gate * (-math.log2(math.e)))
            act_dst[:, cs] = (gate / sig_den * val).astype(BF16)

    def down_stage(act_src):
        acc[...] += jnp.dot(act_src[...], wd_ref[...], preferred_element_type=F32)

    @pl.when(f == 0)
    def _():
        hb[...] = h_ref[...].astype(BF16)
        acc[...] = jnp.zeros(acc.shape, F32)
        up_stage(act_bufs[0])

    for parity in range(2):
        @pl.when((f > 0) & (f < nf) & (f % 2 == parity))
        def _():
            up_stage(act_bufs[parity])
            down_stage(act_bufs[1 - parity])

    @pl.when(f == nf)
    def _():
        act_src = act_bufs[(nf - 1) % 2]
        half = tm // 2
        for rows in (slice(0, half), slice(half, tm)):
            mix = acc[rows, :] + jnp.dot(act_src[rows, :], wd_ref[...],
                                         preferred_element_type=F32)
            out = _layer_norm(DEEPNORM_ALPHA * h_ref[rows, :] + mix, g_ref[...], b_ref[...])
            for o_ref in out_refs:
                o_ref[rows, :] = out.astype(o_ref.dtype)


def _ffn_layer(h2d, w_gv, cp, w_down, ln_g, ln_b, *, layer, seq, bf16_copy, tm=512):
    t, d = h2d.shape
    nf, tf = w_gv.shape[1], w_gv.shape[4]
    fp = nf * tf
    out_dtypes = (F32, BF16) if bf16_copy else (F32,)
    kern = functools.partial(_ffn_kernel, tm=tm, nf=nf, blocks_per_seq=seq // tm,
                             n_out=len(out_dtypes))
    return pl.pallas_call(
        kern,
        grid=(t // tm, nf + 1),
        in_specs=[
            pl.BlockSpec((tm, d), lambda i, f: (i, 0)),
            pl.BlockSpec((None, None, 2, d, tf),
                         lambda i, f: (layer, jnp.minimum(f, nf - 1), 0, 0, 0)),
            _const_spec((SUBLANES, 2 * fp)),
            pl.BlockSpec((None, tf, d), lambda i, f: (layer, jnp.maximum(f - 1, 0), 0)),
            _const_spec((1, d)),
            _const_spec((1, d)),
        ],
        out_specs=[pl.BlockSpec((tm, d), lambda i, f: (i, 0)) for _ in out_dtypes],
        out_shape=[jax.ShapeDtypeStruct((t, d), dt) for dt in out_dtypes],
        scratch_shapes=[
            pltpu.VMEM((tm, d), BF16),
            pltpu.VMEM((tm, d), F32),
            pltpu.VMEM((tf // LANES, SUBLANES + tm, LANES), F32),
            pltpu.VMEM((tf // LANES, SUBLANES + tm, LANES), F32),
            pltpu.VMEM((nf, SUBLANES, tf), F32),
            pltpu.VMEM((nf, SUBLANES, tf), F32),
            pltpu.VMEM((tm, tf), BF16),
            pltpu.VMEM((tm, tf), BF16),
        ],
        compiler_params=pltpu.CompilerParams(
            dimension_semantics=("arbitrary", "arbitrary"), vmem_limit_bytes=VMEM_LIMIT_BYTES),
        name="ffn_layer",
    )(h2d, w_gv, cp, w_down, ln_g, ln_b)


def _lagged_block(nblocks, blocks_per_seq, lag):
    def split(g):
        j = jnp.clip(g - lag, 0, nblocks - 1)
        return j // blocks_per_seq, j % blocks_per_seq
    return split


def _nat_spec(tm, width, split):
    def idx(g):
        b, i = split(g)
        return (b, i, 0)
    return pl.BlockSpec((None, tm, width), idx)


def _dilated_spec(dil, tm, width, split):
    def idx(g):
        b, i = split(g)
        return (b, 0, i, 0)
    return pl.BlockSpec((None, dil, tm // dil, width), idx)


def _proj_kernel(x_ref, w_ref, *refs, nblocks, out_scale):
    nat_ref, o4_ref, o16_ref = refs[:3]
    ybufs, zbuf = refs[3:5], refs[5]
    n_tiles, tm, _ = ybufs[0].shape
    g = pl.program_id(0)
    tiles_per_dot = MXU_DIM // LANES

    def matmul(ybuf, xb, n):
        cols = slice(n * MXU_DIM, (n + 1) * MXU_DIM)
        y = jnp.dot(xb, w_ref[:, cols], preferred_element_type=F32)
        if out_scale != 1.0:
            y = y * out_scale
        nat_ref[0, :, cols] = y.astype(BF16)
        for t in range(tiles_per_dot):
            ybuf[n * tiles_per_dot + t] = y[:, t * LANES:(t + 1) * LANES]

    def scatter(ybuf, n):
        for c in range(n * tiles_per_dot, (n + 1) * tiles_per_dot):
            cs = slice(c * LANES, (c + 1) * LANES)
            for r in range(4):
                slab = ybuf[c, pl.ds(r, tm // 4, stride=4), :]
                o4_ref[r, :, cs] = slab.astype(BF16)
                zbuf[c, r] = slab
            for r in range(4):
                for q in range(4):
                    o16_ref[4 * q + r, :, cs] = (
                        zbuf[c, r, pl.ds(q, tm // 16, stride=4), :].astype(BF16))

    def step(y_dst, y_src):
        xb = None if y_dst is None else x_ref[...].astype(BF16)
        for n in range(n_tiles // tiles_per_dot):
            if y_dst is not None:
                matmul(y_dst, xb, n)
            if y_src is not None:
                scatter(y_src, n)

    @pl.when(g == 0)
    def _():
        step(ybufs[0], None)

    for parity in range(2):
        @pl.when((g > 0) & (g < nblocks) & (g % 2 == parity))
        def _():
            step(ybufs[parity], ybufs[1 - parity])

    @pl.when(g == nblocks)
    def _():
        step(None, ybufs[(nblocks - 1) % 2])


def _proj(h3d, w, *, out_scale=1.0, tm=512):
    bsz, seq, d = h3d.shape
    blocks_per_seq = seq // tm
    nblocks = bsz * blocks_per_seq
    cur = _lagged_block(nblocks, blocks_per_seq, 0)
    prev = _lagged_block(nblocks, blocks_per_seq, 1)
    assert DILATIONS == (1, 4, 16), "the two-step row de-interleave is written for 1, 4, 16"
    ybuf = pltpu.VMEM((d // LANES, tm, LANES), F32)
    zbuf = pltpu.VMEM((d // LANES, 4, tm // 4, LANES), F32)
    return pl.pallas_call(
        functools.partial(_proj_kernel, nblocks=nblocks, out_scale=out_scale),
        grid=(nblocks + 1,),
        in_specs=[_nat_spec(tm, d, cur), _const_spec((d, d))],
        out_specs=[_dilated_spec(1, tm, d, cur), _dilated_spec(4, tm, d, prev),
                   _dilated_spec(16, tm, d, prev)],
        out_shape=[jax.ShapeDtypeStruct((bsz, dil, seq // dil, d), BF16) for dil in DILATIONS],
        scratch_shapes=[ybuf, ybuf, zbuf],
        compiler_params=pltpu.CompilerParams(
            dimension_semantics=("arbitrary",), vmem_limit_bytes=VMEM_LIMIT_BYTES),
        name="proj",
    )(h3d, w)


def _attn_kernel(q_ref, *refs, qb, n_sub, n_heads, heads_per_iter, halo):
    if halo:
        kp_ref, kc_ref, vp_ref, vc_ref, o_ref, lse_ref, bias = refs
    else:
        kc_ref, vc_ref, o_ref, lse_ref, bias = refs
    i = pl.program_id(2)
    nblk = qb // ATTN_BLOCK
    blk = ATTN_BLOCK

    row = lax.broadcasted_iota(jnp.int32, (blk, 2 * blk), 0)
    col = lax.broadcasted_iota(jnp.int32, (blk, 2 * blk), 1)
    dist = blk + row - col
    band = (dist >= 0) & (dist <= blk)
    has_prev = jnp.broadcast_to(i > 0, band.shape)
    bias[1] = jnp.where(band, 0.0, NEG_INF)
    bias[0] = jnp.where(band & ((col >= blk) | has_prev), 0.0, NEG_INF)

    lane = lax.broadcasted_iota(jnp.int32, (blk, LANES), 1)
    ones = jnp.ones((2 * blk, HEAD_DIM), BF16)
    lse_ref[...] = jnp.zeros(lse_ref.shape, F32)

    def one_block(sub, h, j):
        hc = pl.ds(pl.multiple_of(h * HEAD_DIM, HEAD_DIM), HEAD_DIM)
        rows = slice(j * blk, (j + 1) * blk)
        q = q_ref[sub, rows, hc]
        if j > 0:
            kk = kc_ref[sub, (j - 1) * blk:(j + 1) * blk, hc]
            vv = vc_ref[sub, (j - 1) * blk:(j + 1) * blk, hc]
        else:
            k_prev = kp_ref[:, hc] if halo else kc_ref[sub, 0:blk, hc]
            v_prev = vp_ref[:, hc] if halo else vc_ref[sub, 0:blk, hc]
            kk = jnp.concatenate([k_prev, kc_ref[sub, 0:blk, hc]], axis=0)
            vv = jnp.concatenate([v_prev, vc_ref[sub, 0:blk, hc]], axis=0)
        s = lax.dot_general(q, kk, (((1,), (1,)), ((), ())), preferred_element_type=F32)
        s = s + bias[min(j, 1)]
        m = jnp.max(s, axis=-1, keepdims=True)
        p = jnp.exp2(s - m)
        ov = jnp.dot(p.astype(BF16), jnp.concatenate([vv, ones], axis=1),
                     preferred_element_type=F32)
        den = ov[:, HEAD_DIM:]
        o_ref[sub, rows, hc] = (ov[:, :HEAD_DIM] / den).astype(o_ref.dtype)
        lse = m + jnp.log2(den)
        lse_ref[sub, rows, :] = jnp.where(lane == h, lse, lse_ref[sub, rows, :])

    def head_group(g, carry):
        for hh in range(heads_per_iter):
            for sub in range(n_sub):
                for j in range(nblk):
                    one_block(sub, g * heads_per_iter + hh, j)
        return carry

    lax.fori_loop(0, n_heads // heads_per_iter, head_group, 0)


def _attn_branch(q, k, v):
    bsz, dil, length, d = q.shape
    n_heads = d // HEAD_DIM
    qb = min(length, ATTN_ROWS_PER_STEP)
    halo = length > qb
    n_sub = 1 if halo else min(dil, ATTN_ROWS_PER_STEP // qb)
    per = qb // ATTN_BLOCK
    heads_per_iter = min(n_heads, max(1, ATTN_BLOCKS_PER_ITER // (per * n_sub)))
    cur = lambda width: pl.BlockSpec((None, n_sub, qb, width), lambda b, r, i: (b, r, i, 0))
    prev = pl.BlockSpec((None, None, ATTN_BLOCK, d),
                        lambda b, r, i: (b, r, jnp.maximum(i * per - 1, 0), 0))
    kern = functools.partial(_attn_kernel, qb=qb, n_sub=n_sub, n_heads=n_heads,
                             heads_per_iter=heads_per_iter, halo=halo)
    return pl.pallas_call(
        kern,
        grid=(bsz, dil // n_sub, length // qb),
        in_specs=[cur(d), prev, cur(d), prev, cur(d)] if halo else [cur(d)] * 3,
        out_specs=[cur(d), cur(LANES)],
        out_shape=[
            jax.ShapeDtypeStruct((bsz, dil, length, d), BF16),
            jax.ShapeDtypeStruct((bsz, dil, length, LANES), F32),
        ],
        scratch_shapes=[pltpu.VMEM((2, ATTN_BLOCK, 2 * ATTN_BLOCK), F32)],
        compiler_params=pltpu.CompilerParams(
            dimension_semantics=("arbitrary", "arbitrary", "arbitrary"),
            vmem_limit_bytes=VMEM_LIMIT_BYTES),
        name=f"attn_d{dil}",
    )(*((q, k, k, v, v) if halo else (q, k, v)))


def _attn_out_kernel(o1_ref, o4_ref, o16_ref, l1_ref, l4_ref, l16_ref, h_ref, wo_ref, g_ref,
                     b_ref, out_ref, o_nat, z16, l_nat, zl16, ybuf, *xbufs, n_heads, nblocks):
    tm = h_ref.shape[0]
    g = pl.program_id(0)
    heads_per_dot = MXU_DIM // HEAD_DIM

    def to_natural(dst, src4):
        for r in range(4):
            dst[pl.ds(r, tm // 4, stride=4), :] = src4(r)

    def from_d16(tmp, src16):
        for r in range(4):
            for q in range(4):
                tmp[r, pl.ds(q, tm // 16, stride=4), :] = src16(4 * q + r)
        return lambda r: tmp[r]

    def merge_weights():
        to_natural(l_nat.at[0], lambda r: l4_ref[r])
        to_natural(l_nat.at[1], from_d16(zl16, lambda r: l16_ref[r]))
        lses = [l1_ref[0], l_nat[0], l_nat[1]]
        mx = functools.reduce(jnp.maximum, lses)
        es = [jnp.exp2(l - mx) for l in lses]
        tot = functools.reduce(lambda a, b: a + b, es)
        return es[0] / tot, es[1] / tot

    def merge_head(xbuf, w1, w4, h):
        hc = slice(h * HEAD_DIM, (h + 1) * HEAD_DIM)
        to_natural(o_nat.at[0, h], lambda r: o4_ref[r, :, hc].astype(F32))
        to_natural(o_nat.at[1, h],
                   from_d16(z16.at[h], lambda r: o16_ref[r, :, hc].astype(F32)))
        o1 = o1_ref[0, :, hc].astype(F32)
        o4 = o_nat[0, h]
        o16 = o_nat[1, h]
        shape = (tm, HEAD_DIM)
        o = (o16 + jnp.broadcast_to(w1[:, h:h + 1], shape) * (o1 - o16)
             + jnp.broadcast_to(w4[:, h:h + 1], shape) * (o4 - o16))
        xbuf[:, hc] = o.astype(BF16)

    def step(x_dst, x_src):
        if x_dst is not None:
            w1, w4 = merge_weights()
        for n in range(n_heads // heads_per_dot):
            if x_src is not None:
                cols = slice(n * MXU_DIM, (n + 1) * MXU_DIM)
                ybuf[:, cols] = jnp.dot(x_src[...], wo_ref[:, cols], preferred_element_type=F32)
            if x_dst is not None:
                for h in range(n * heads_per_dot, (n + 1) * heads_per_dot):
                    merge_head(x_dst, w1, w4, h)
        if x_src is not None:
            y = DEEPNORM_ALPHA * h_ref[...] + ybuf[...]
            out_ref[...] = _layer_norm(y, g_ref[...], b_ref[...])

    @pl.when(g == 0)
    def _():
        step(xbufs[0], None)

    for parity in range(2):
        @pl.when((g > 0) & (g < nblocks) & (g % 2 == parity))
        def _():
            step(xbufs[parity], xbufs[1 - parity])

    @pl.when(g == nblocks)
    def _():
        step(None, xbufs[(nblocks - 1) % 2])


def _attn_out(os, ls, h3d, w_o, ln_g, ln_b, *, tm=256):
    bsz, seq, d = h3d.shape
    assert DILATIONS == (1, 4, 16), "the two-step row interleave is written for 1, 4, 16"
    n_heads = d // HEAD_DIM
    blocks_per_seq = seq // tm
    nblocks = bsz * blocks_per_seq
    cur = _lagged_block(nblocks, blocks_per_seq, 0)
    prev = _lagged_block(nblocks, blocks_per_seq, 1)
    xbuf = pltpu.VMEM((tm, d), BF16)
    kern = functools.partial(_attn_out_kernel, n_heads=n_heads, nblocks=nblocks)
    return pl.pallas_call(
        kern,
        grid=(nblocks + 1,),
        in_specs=([_dilated_spec(dil, tm, d, cur) for dil in DILATIONS]
                  + [_dilated_spec(dil, tm, LANES, cur) for dil in DILATIONS]
                  + [_nat_spec(tm, d, prev), _const_spec((d, d)), _const_spec((1, d)),
                     _const_spec((1, d))]),
        out_specs=_nat_spec(tm, d, prev),
        out_shape=jax.ShapeDtypeStruct((bsz, seq, d), F32),
        scratch_shapes=[
            pltpu.VMEM((2, n_heads, tm, HEAD_DIM), F32),
            pltpu.VMEM((n_heads, 4, tm // 4, HEAD_DIM), F32),
            pltpu.VMEM((2, tm, LANES), F32),
            pltpu.VMEM((4, tm // 4, LANES), F32),
            pltpu.VMEM((tm, d), F32),
            xbuf, xbuf,
        ],
        compiler_params=pltpu.CompilerParams(
            dimension_semantics=("arbitrary",), vmem_limit_bytes=VMEM_LIMIT_BYTES),
        name="attn_out",
    )(*os, *ls, h3d, w_o, ln_g, ln_b)


def _cast_pad_rows_kernel(x_ref, o_ref):
    n = x_ref.shape[0]
    o_ref[0:n, :] = x_ref[...].astype(BF16)
    if o_ref.shape[0] > n:
        o_ref[n:, :] = jnp.zeros((o_ref.shape[0] - n, o_ref.shape[1]), BF16)


def _cast_chunks_kernel(x_ref, o_ref):
    nf, _, tf = o_ref.shape
    f = x_ref.shape[1]
    for c in range(nf):
        width = min(tf, f - c * tf)
        o_ref[c, :, :width] = x_ref[:, c * tf:c * tf + width].astype(BF16)
        if width < tf:
            o_ref[c, :, width:] = jnp.zeros((o_ref.shape[1], tf - width), BF16)


def _ffn_weights(w_up, w_down, tf, *, rows=256, cols=256):
    n_layers, d, two_f = w_up.shape
    f = two_f // 2
    fp = f + (-f % tf)
    nf = fp // tf
    params = pltpu.CompilerParams(dimension_semantics=("arbitrary",) * 3,
                                  vmem_limit_bytes=VMEM_LIMIT_BYTES)
    w_gv = pl.pallas_call(
        _cast_chunks_kernel,
        grid=(n_layers, d // rows, 2),
        in_specs=[pl.BlockSpec((None, rows, f), lambda l, r, j: (l, r, j))],
        out_specs=pl.BlockSpec((None, nf, None, rows, tf), lambda l, r, j: (l, 0, j, r, 0)),
        out_shape=jax.ShapeDtypeStruct((n_layers, nf, 2, d, tf), BF16),
        compiler_params=params,
        name="ffn_w_up_bf16",
    )(w_up)
    w_dn = pl.pallas_call(
        _cast_pad_rows_kernel,
        grid=(n_layers, d // cols, 1),
        in_specs=[pl.BlockSpec((None, f, cols), lambda l, c, _: (l, 0, c))],
        out_specs=pl.BlockSpec((None, fp, cols), lambda l, c, _: (l, 0, c)),
        out_shape=jax.ShapeDtypeStruct((n_layers, fp, d), BF16),
        compiler_params=params,
        name="ffn_w_down_bf16",
    )(w_down)
    return w_gv, w_dn, fp


def _conv_params(conv_w, conv_b, fp):
    two_f = conv_w.shape[1]
    f = two_f // 2
    fill = jnp.zeros((SUBLANES - CONV_WIDTH - 1, two_f), F32)
    cp = jnp.concatenate([conv_w, conv_b[None, :], fill], axis=0)
    z = jnp.zeros((SUBLANES, fp - f), F32)
    return jnp.concatenate([cp[:, :f], z, cp[:, f:], z], axis=1)


def kernel(x, pool_w_in, pool_w_grp, pool_scale, pool_w_out, attn_w_q, attn_w_o, shared_w_k,
           shared_w_v, ffn_w_up, ffn_conv_w, ffn_conv_b, ffn_w_down, ln1_g, ln1_b, ln2_g, ln2_b):
    bsz, seq, d = x.shape
    n_a = pool_w_in.shape[0]
    n_layers = ffn_w_up.shape[0]
    tf = 2 * MXU_DIM
    row = lambda a: a.reshape(1, d)

    w_gv, w_dn, fp = _ffn_weights(ffn_w_up, ffn_w_down, tf)
    h = x.reshape(bsz * seq, d)
    h_bf = None
    kv = None
    for i in range(n_layers):
        if i < n_a:
            h = _pool_layer(h, pool_w_in[i].astype(BF16), pool_w_grp[i].astype(BF16),
                            row(pool_scale[i]), pool_w_out[i].astype(BF16),
                            row(ln1_g[i]), row(ln1_b[i]), seq=seq)
        else:
            h3d = h.reshape(bsz, seq, d)
            x3d = h3d if h_bf is None else h_bf.reshape(bsz, seq, d)
            if kv is None:
                kv = (_proj(x3d, shared_w_k.astype(BF16)), _proj(x3d, shared_w_v.astype(BF16)))
            qs = _proj(x3d, attn_w_q[i - n_a].astype(BF16), out_scale=QK_SCALE_LOG2)
            outs = [_attn_branch(q, k, v) for q, k, v in zip(qs, kv[0], kv[1])]
            h = _attn_out([o for o, _ in outs], [l for _, l in outs], h3d,
                          attn_w_o[i - n_a].astype(BF16), row(ln1_g[i]), row(ln1_b[i]))
            h = h.reshape(bsz * seq, d)
        cp = _conv_params(ffn_conv_w[i], ffn_conv_b[i], fp)
        feeds_attention = n_a <= i + 1 < n_layers
        outs = _ffn_layer(h, w_gv, cp, w_dn, row(ln2_g[i]), row(ln2_b[i]), layer=i, seq=seq,
                          bf16_copy=feeds_attention)
        h, h_bf = (outs[0], outs[1]) if feeds_attention else (outs[0], None)
    return h.reshape(bsz, seq, d)
```
